```python
import math
import jax, jax.numpy as jnp
from jax import lax
import numpy as np

D_MODEL = 2048
BATCH = 8
SEQ = 4096
DEPTH = 2

N_A_LAYERS = DEPTH // 2
N_B_LAYERS = DEPTH - N_A_LAYERS
BLOCK_Q = 128
EPS = 1e-6

SB_HEADS = 16
SB_HEAD_DIM = D_MODEL // SB_HEADS

MLA_HEADS = 16
QK_NOPE = 128
QK_ROPE = 64
V_HEAD = 128
Q_LORA = 512
KV_LORA = 512
ROPE_BASE = 10000.0

PEER_HEADS = 8
N_KEYS = 128
N_EXPERTS = N_KEYS * N_KEYS
PEER_TOPK = 16
PEER_KEY_DIM = 256
PEER_HALF = PEER_KEY_DIM // 2
PEER_CHUNK = 128

kernel_name = 'hybrid_yoco_stickbreak_mla_peer'


def rmsnorm(x, g):
    xf = x.astype(jnp.float32)
    y = xf * lax.rsqrt(jnp.mean(xf * xf, axis=-1, keepdims=True) + EPS)
    return (y * g.astype(jnp.float32)).astype(x.dtype)


def rope(x, positions):
    r = x.shape[-1]
    inv_freq = ROPE_BASE ** (-jnp.arange(0, r, 2, dtype=jnp.float32) / r)
    ang = positions.astype(jnp.float32)[..., None] * inv_freq
    cos = jnp.cos(ang)[:, :, None, :]
    sin = jnp.sin(ang)[:, :, None, :]
    xf = x.astype(jnp.float32)
    x1, x2 = xf[..., : r // 2], xf[..., r // 2:]
    out = jnp.concatenate([x1 * cos - x2 * sin, x2 * cos + x1 * sin], axis=-1)
    return out.astype(x.dtype)


def stick_breaking_attention(h, w_qkv, w_o):
    b, s, _ = h.shape
    q, k, v = jnp.split(h @ w_qkv, 3, axis=-1)
    q = q.reshape(b, s, SB_HEADS, SB_HEAD_DIM)
    k = k.reshape(b, s, SB_HEADS, SB_HEAD_DIM)
    v = v.reshape(b, s, SB_HEADS, SB_HEAD_DIM)
    scale = SB_HEAD_DIM ** -0.5
    outs = []
    for i in range(s // BLOCK_Q):
        lo, hi = i * BLOCK_Q, (i + 1) * BLOCK_Q
        z = jnp.einsum('bqhd,bkhd->bhqk', q[:, lo:hi], k[:, :hi]).astype(jnp.float32) * scale
        causal = jnp.arange(hi)[None, :] < jnp.arange(lo, hi)[:, None]
        log_1m_beta = jnp.where(causal, jax.nn.log_sigmoid(-z), 0.0)
        tail = lax.cumsum(log_1m_beta, axis=3, reverse=True) - log_1m_beta
        a = jnp.where(causal, jnp.exp(jax.nn.log_sigmoid(z) + tail), 0.0)
        outs.append(jnp.einsum('bhqk,bkhd->bqhd', a.astype(v.dtype), v[:, :hi]))
    o = jnp.concatenate(outs, axis=1).reshape(b, s, SB_HEADS * SB_HEAD_DIM)
    return o @ w_o


def mla_shared_kv(x, positions, kv_norm_g, w_dkv, kv_latent_g, w_ukv):
    b, s, _ = x.shape
    h = rmsnorm(x, kv_norm_g)
    ckv = h @ w_dkv
    c = rmsnorm(ckv[..., :KV_LORA], kv_latent_g)
    k_rope = rope(ckv[..., None, KV_LORA:], positions)[:, :, 0, :]
    kv = (c @ w_ukv).reshape(b, s, MLA_HEADS, QK_NOPE + V_HEAD)
    return kv[..., :QK_NOPE], k_rope, kv[..., QK_NOPE:]


def mla_attention(h, positions, k_nope, k_rope, v, w_dq, q_latent_g, w_uq, w_o):
    b, s, _ = h.shape
    cq = rmsnorm(h @ w_dq, q_latent_g)
    q = (cq @ w_uq).reshape(b, s, MLA_HEADS, QK_NOPE + QK_ROPE)
    q_nope = q[..., :QK_NOPE]
    q_rope = rope(q[..., QK_NOPE:], positions)
    scale = (QK_NOPE + QK_ROPE) ** -0.5
    outs = []
    for i in range(s // BLOCK_Q):
        lo, hi = i * BLOCK_Q, (i + 1) * BLOCK_Q
        sc = (jnp.einsum('bqhd,bkhd->bhqk', q_nope[:, lo:hi], k_nope[:, :hi])
              + jnp.einsum('bqhr,bkr->bhqk', q_rope[:, lo:hi], k_rope[:, :hi])).astype(jnp.float32) * scale
        causal = jnp.arange(hi)[None, :] <= jnp.arange(lo, hi)[:, None]
        p = jax.nn.softmax(jnp.where(causal, sc, -jnp.inf), axis=-1)
        outs.append(jnp.einsum('bhqk,bkhd->bqhd', p.astype(v.dtype), v[:, :hi]))
    o = jnp.concatenate(outs, axis=1).reshape(b, s, MLA_HEADS * V_HEAD)
    return o @ w_o


def peer_ffn(h, w_q, keys_1, keys_2, expert_u, expert_v):
    b, s, d = h.shape
    q = (h @ w_q).reshape(b, s, PEER_HEADS, 2, PEER_HALF)
    s1 = jnp.einsum('bshd,hkd->bshk', q[..., 0, :], keys_1).astype(jnp.float32)
    s2 = jnp.einsum('bshd,hkd->bshk', q[..., 1, :], keys_2).astype(jnp.float32)
    v1, i1 = lax.top_k(s1, PEER_TOPK)
    v2, i2 = lax.top_k(s2, PEER_TOPK)
    cand_sc = (v1[..., :, None] + v2[..., None, :]).reshape(b, s, PEER_HEADS, PEER_TOPK * PEER_TOPK)
    cand_id = (i1[..., :, None] * N_KEYS + i2[..., None, :]).reshape(b, s, PEER_HEADS, PEER_TOPK * PEER_TOPK)
    top_sc, pos = lax.top_k(cand_sc, PEER_TOPK)
    ids = jnp.take_along_axis(cand_id, pos, axis=-1)
    g = jax.nn.softmax(top_sc, axis=-1).astype(h.dtype)
    n_sel = PEER_HEADS * PEER_TOPK
    n_chunks = (b * s) // PEER_CHUNK
    hc = h.reshape(n_chunks, PEER_CHUNK, d)
    idc = ids.reshape(n_chunks, PEER_CHUNK, n_sel)
    gc = g.reshape(n_chunks, PEER_CHUNK, n_sel)

    def chunk_fn(args):
        hx, idx, gx = args
        u = jnp.take(expert_u, idx, axis=0)
        act = jax.nn.gelu(jnp.einsum('ced,cd->ce', u, hx), approximate=False)
        vv = jnp.take(expert_v, idx, axis=0)
        return jnp.einsum('ce,ced->cd', gx * act, vv)

    out = lax.map(chunk_fn, (hc, idc, gc))
    return out.reshape(b, s, d)


def setup_inputs(seed: int = 0) -> dict:
    key = jax.random.key(seed)
    ks = jax.random.split(key, 24)

    def nrm(k, shape, scale):
        return jax.random.normal(k, shape, jnp.float32) * scale

    def gain(k, shape):
        return 1.0 + 0.01 * jax.random.normal(k, shape, jnp.float32)

    x = jax.random.normal(ks[0], (BATCH, SEQ, D_MODEL), jnp.float32)
    offsets = jax.random.randint(ks[1], (BATCH, 1), 0, 1024, dtype=jnp.int32)
    positions = (jnp.arange(SEQ, dtype=jnp.int32)[None, :] + offsets).astype(jnp.int32)
    return {
        'x': x,
        'positions': positions,
        'sb_norm_g': gain(ks[2], (N_A_LAYERS, D_MODEL)),
        'sb_w_qkv': nrm(ks[3], (N_A_LAYERS, D_MODEL, 3 * SB_HEADS * SB_HEAD_DIM), D_MODEL ** -0.5),
        'sb_w_o': nrm(ks[4], (N_A_LAYERS, SB_HEADS * SB_HEAD_DIM, D_MODEL), (SB_HEADS * SB_HEAD_DIM) ** -0.5),
        'kv_norm_g': gain(ks[5], (D_MODEL,)),
        'mla_w_dkv': nrm(ks[6], (D_MODEL, KV_LORA + QK_ROPE), D_MODEL ** -0.5),
        'mla_kv_latent_g': gain(ks[7], (KV_LORA,)),
        'mla_w_ukv': nrm(ks[8], (KV_LORA, MLA_HEADS * (QK_NOPE + V_HEAD)), KV_LORA ** -0.5),
        'mla_norm_g': gain(ks[9], (N_B_LAYERS, D_MODEL)),
        'mla_w_dq': nrm(ks[10], (N_B_LAYERS, D_MODEL, Q_LORA), D_MODEL ** -0.5),
        'mla_q_latent_g': gain(ks[11], (N_B_LAYERS, Q_LORA)),
        'mla_w_uq': nrm(ks[12], (N_B_LAYERS, Q_LORA, MLA_HEADS * (QK_NOPE + QK_ROPE)), Q_LORA ** -0.5),
        'mla_w_o': nrm(ks[13], (N_B_LAYERS, MLA_HEADS * V_HEAD, D_MODEL), (MLA_HEADS * V_HEAD) ** -0.5),
        'peer_norm_g': gain(ks[14], (DEPTH, D_MODEL)),
        'peer_w_q': nrm(ks[15], (DEPTH, D_MODEL, PEER_HEADS * PEER_KEY_DIM), D_MODEL ** -0.5),
        'peer_keys_1': nrm(ks[16], (DEPTH, PEER_HEADS, N_KEYS, PEER_HALF), PEER_HALF ** -0.5),
        'peer_keys_2': nrm(ks[17], (DEPTH, PEER_HEADS, N_KEYS, PEER_HALF), PEER_HALF ** -0.5),
        'peer_u': nrm(ks[18], (DEPTH, N_EXPERTS, D_MODEL), D_MODEL ** -0.5),
        'peer_v': nrm(ks[19], (DEPTH, N_EXPERTS, D_MODEL), PEER_HEADS ** -0.5),
        'final_norm_g': gain(ks[20], (D_MODEL,)),
    }


def reference(x, positions, sb_norm_g, sb_w_qkv, sb_w_o, kv_norm_g, mla_w_dkv, mla_kv_latent_g,
              mla_w_ukv, mla_norm_g, mla_w_dq, mla_q_latent_g, mla_w_uq, mla_w_o, peer_norm_g,
              peer_w_q, peer_keys_1, peer_keys_2, peer_u, peer_v, final_norm_g):
    k_nope = k_rope = v_shared = None
    for layer in range(DEPTH):
        if layer < N_A_LAYERS:
            x = x + stick_breaking_attention(rmsnorm(x, sb_norm_g[layer]), sb_w_qkv[layer], sb_w_o[layer])
        else:
            if layer == N_A_LAYERS:
                k_nope, k_rope, v_shared = mla_shared_kv(x, positions, kv_norm_g, mla_w_dkv,
                                                         mla_kv_latent_g, mla_w_ukv)
            j = layer - N_A_LAYERS
            x = x + mla_attention(rmsnorm(x, mla_norm_g[j]), positions, k_nope, k_rope, v_shared,
                                  mla_w_dq[j], mla_q_latent_g[j], mla_w_uq[j], mla_w_o[j])
        x = x + peer_ffn(rmsnorm(x, peer_norm_g[layer]), peer_w_q[layer], peer_keys_1[layer],
                         peer_keys_2[layer], peer_u[layer], peer_v[layer])
    return rmsnorm(x, final_norm_g)
```

```python
import functools
import math

import jax
import jax.numpy as jnp
from jax import lax
from jax.experimental import pallas as pl
from jax.experimental.pallas import tpu as pltpu

F32 = jnp.float32
BF16 = jnp.bfloat16

EPS = 1e-6
LANES = 128
HEAD_DIM = 128
ROPE_DIM = 64
ROPE_HALF = ROPE_DIM // 2
ROPE_BASE = 10000.0
ROPE_GROUP = LANES // ROPE_HALF
PEER_TOPK = 16
PEER_KEYS = 128
PEER_HALF = 128

ATTN_TILE = 256
ROW_TILE = 512
ROUTE_TILE = 512
GATHER_TOKENS = 8
VMEM_LIMIT = 56 * 1024 * 1024


def _cparams(*sem):
    return pltpu.CompilerParams(dimension_semantics=sem, vmem_limit_bytes=VMEM_LIMIT)


def _fit_tile(size, tile):
    tile = min(tile, size)
    while size % tile:
        tile //= 2
    return tile


def _dot(a, b):
    return jnp.dot(a, b, preferred_element_type=F32)


def _dot_nt(a, b):
    return lax.dot_general(a, b, (((1,), (1,)), ((), ())), preferred_element_type=F32)


def _norm_matmul_kernel(x_ref, g_ref, w_ref, o_ref, xn_ref):
    @pl.when(pl.program_id(1) == 0)
    def _():
        x = x_ref[...]
        ms = jnp.mean(x * x, axis=-1, keepdims=True)
        xn_ref[...] = (x * lax.rsqrt(ms + EPS) * g_ref[...]).astype(BF16)

    o_ref[...] = _dot(xn_ref[...], w_ref[...]).astype(o_ref.dtype)


def _norm_matmul(x, g, w, out_dtype, *, k=None, tn=512):
    n = x.shape[0]
    k = x.shape[1] if k is None else k
    f = w.shape[1]
    tm = _fit_tile(n, ROW_TILE)
    tn = _fit_tile(f, tn)
    assert w.shape[0] == k
    return pl.pallas_call(
        _norm_matmul_kernel,
        grid=(n // tm, f // tn),
        in_specs=[
            pl.BlockSpec((tm, k), lambda i, j: (i, 0)),
            pl.BlockSpec((1, k), lambda i, j: (0, 0)),
            pl.BlockSpec((k, tn), lambda i, j: (0, j)),
        ],
        out_specs=pl.BlockSpec((tm, tn), lambda i, j: (i, j)),
        out_shape=jax.ShapeDtypeStruct((n, f), out_dtype),
        scratch_shapes=[pltpu.VMEM((tm, k), BF16)],
        compiler_params=_cparams("parallel", "arbitrary"),
        name="norm_matmul",
    )(x, g.reshape(1, k).astype(F32), w.astype(BF16))


def _matmul_residual_kernel(a_ref, w_ref, x_ref, o_ref):
    o_ref[...] = x_ref[...] + _dot(a_ref[...], w_ref[...])


def _matmul_residual(a, w, x, *, tn=512):
    n, k = a.shape
    f = w.shape[1]
    tm = _fit_tile(n, ROW_TILE)
    tn = _fit_tile(f, tn)
    return pl.pallas_call(
        _matmul_residual_kernel,
        grid=(n // tm, f // tn),
        in_specs=[
            pl.BlockSpec((tm, k), lambda i, j: (i, 0)),
            pl.BlockSpec((k, tn), lambda i, j: (0, j)),
            pl.BlockSpec((tm, tn), lambda i, j: (i, j)),
        ],
        out_specs=pl.BlockSpec((tm, tn), lambda i, j: (i, j)),
        out_shape=jax.ShapeDtypeStruct((n, f), F32),
        compiler_params=_cparams("parallel", "arbitrary"),
        name="matmul_residual",
    )(a, w.astype(BF16), x)


def _rmsnorm_kernel(x_ref, g_ref, o_ref):
    x = x_ref[...]
    ms = jnp.mean(x * x, axis=-1, keepdims=True)
    o_ref[...] = x * lax.rsqrt(ms + EPS) * g_ref[...]


def _rmsnorm(x, g):
    n, d = x.shape
    tm = min(ROW_TILE, n)
    return pl.pallas_call(
        _rmsnorm_kernel,
        grid=(n // tm,),
        in_specs=[pl.BlockSpec((tm, d), lambda i: (i, 0)), pl.BlockSpec((1, d), lambda i: (0, 0))],
        out_specs=pl.BlockSpec((tm, d), lambda i: (i, 0)),
        out_shape=jax.ShapeDtypeStruct((n, d), F32),
        compiler_params=_cparams("parallel"),
        name="final_rmsnorm",
    )(x, g.reshape(1, d))


def _sb_attn_kernel(q_ref, k_ref, v_ref, o_ref, *, tile, scale):
    i = pl.program_id(2)
    q = q_ref[...]
    row = lax.broadcasted_iota(jnp.int32, (tile, tile), 0)
    col = lax.broadcasted_iota(jnp.int32, (tile, tile), 1)
    later = (row > col).astype(BF16)
    causal = col < row

    def visit(j, carry, acc, masked):
        start = pl.multiple_of(j * tile, tile)
        ks = k_ref[pl.ds(start, tile), :]
        vs = v_ref[pl.ds(start, tile), :]
        z = _dot_nt(q, ks) * scale
        softplus = jnp.maximum(z, 0.0) + jnp.log(1.0 + jnp.exp(-jnp.abs(z)))
        log_1m_beta = -softplus
        log_beta = z - softplus
        if masked:
            log_1m_beta = jnp.where(causal, log_1m_beta, 0.0)
        hi = log_1m_beta.astype(BF16)
        lo = (log_1m_beta - hi.astype(F32)).astype(BF16)
        tail = _dot(hi, later) + _dot(lo, later)
        a = jnp.exp(log_beta + tail + carry)
        if masked:
            a = jnp.where(causal, a, 0.0)
        acc = acc + _dot(a.astype(BF16), vs)
        carry = carry + tail[:, 0:1] + log_1m_beta[:, 0:1]
        return carry, acc

    carry0 = jnp.zeros((tile, 1), F32)
    acc0 = jnp.zeros((tile, HEAD_DIM), F32)
    carry, acc = visit(i, carry0, acc0, True)

    def body(n, state):
        return visit(i - 1 - n, state[0], state[1], False)

    carry, acc = lax.fori_loop(0, i, body, (carry, acc))
    o_ref[...] = acc.astype(o_ref.dtype)


def _sb_attention(qkv, heads):
    b, s, _ = qkv.shape
    tile = min(ATTN_TILE, s)
    assert s % tile == 0
    kern = functools.partial(_sb_attn_kernel, tile=tile, scale=HEAD_DIM ** -0.5)
    return pl.pallas_call(
        kern,
        grid=(b, heads, s // tile),
        in_specs=[
            pl.BlockSpec((None, tile, HEAD_DIM), lambda bi, h, i: (bi, i, h)),
            pl.BlockSpec((None, s, HEAD_DIM), lambda bi, h, i: (bi, 0, heads + h)),
            pl.BlockSpec((None, s, HEAD_DIM), lambda bi, h, i: (bi, 0, 2 * heads + h)),
        ],
        out_specs=pl.BlockSpec((None, tile, HEAD_DIM), lambda bi, h, i: (bi, i, h)),
        out_shape=jax.ShapeDtypeStruct((b, s, heads * HEAD_DIM), BF16),
        compiler_params=_cparams("parallel", "parallel", "arbitrary"),
        name="sb_attention",
    )(qkv, qkv, qkv)


def _rope_kernel(x1_ref, x2_ref, pos_ref, freq_ref, o1_ref, o2_ref):
    ang = pos_ref[...] * freq_ref[...]
    c = jnp.cos(ang)
    s = jnp.sin(ang)
    x1 = x1_ref[...].astype(F32)
    x2 = x2_ref[...].astype(F32)
    o1_ref[...] = (x1 * c - x2 * s).astype(o1_ref.dtype)
    o2_ref[...] = (x2 * c + x1 * s).astype(o2_ref.dtype)


def _rope(src, col1, col2, nblk, pos, freq):
    n = src.shape[0]
    tm = min(ROW_TILE, n)
    out = jax.ShapeDtypeStruct((n, nblk * LANES), BF16)
    return pl.pallas_call(
        _rope_kernel,
        grid=(n // tm, nblk),
        in_specs=[
            pl.BlockSpec((tm, LANES), lambda i, j: (i, col1 + j)),
            pl.BlockSpec((tm, LANES), lambda i, j: (i, col2 + j)),
            pl.BlockSpec((tm, 1), lambda i, j: (i, 0)),
            pl.BlockSpec((1, LANES), lambda i, j: (0, 0)),
        ],
        out_specs=[pl.BlockSpec((tm, LANES), lambda i, j: (i, j))] * 2,
        out_shape=[out, out],
        compiler_params=_cparams("parallel", "parallel"),
        name="rope",
    )(src, src, pos, freq)


def _mla_attn_kernel(qn_ref, qr1_ref, qr2_ref, kn_ref, v_ref, kr1_ref, kr2_ref, o_ref, *, tile, scale):
    h = pl.program_id(1)
    i = pl.program_id(2)
    lane = lax.broadcasted_iota(jnp.int32, (1, LANES), 1)
    mine = (lane // ROPE_HALF) == (h % ROPE_GROUP)
    qn = qn_ref[...]
    qr1 = jnp.where(mine, qr1_ref[...], jnp.zeros((), BF16))
    qr2 = jnp.where(mine, qr2_ref[...], jnp.zeros((), BF16))
    row = lax.broadcasted_iota(jnp.int32, (tile, tile), 0)
    col = lax.broadcasted_iota(jnp.int32, (tile, tile), 1)
    causal = col <= row

    def visit(j, m, l, acc, masked):
        start = pl.multiple_of(j * tile, tile)
        sl = pl.ds(start, tile)
        sc = (_dot_nt(qn, kn_ref[sl, :]) + _dot_nt(qr1, kr1_ref[sl, :]) + _dot_nt(qr2, kr2_ref[sl, :])) * scale
        if masked:
            sc = jnp.where(causal, sc, -jnp.inf)
        m_new = jnp.maximum(m, jnp.max(sc, axis=-1, keepdims=True))
        p = jnp.exp(sc - m_new)
        alpha = jnp.exp(m - m_new)
        l = alpha * l + jnp.sum(p, axis=-1, keepdims=True)
        acc = alpha * acc + _dot(p.astype(BF16), v_ref[sl, :])
        return m_new, l, acc

    m0 = jnp.full((tile, 1), -jnp.inf, F32)
    l0 = jnp.zeros((tile, 1), F32)
    acc0 = jnp.zeros((tile, HEAD_DIM), F32)
    state = visit(i, m0, l0, acc0, True)

    def body(n, st):
        return visit(n, st[0], st[1], st[2], False)

    m, l, acc = lax.fori_loop(0, i, body, state)
    o_ref[...] = (acc / l).astype(o_ref.dtype)


def _mla_attention(qn, qr1, qr2, kv, kr1, kr2, heads):
    b, s, _ = qn.shape
    tile = min(ATTN_TILE, s)
    kern = functools.partial(_mla_attn_kernel, tile=tile, scale=(HEAD_DIM + ROPE_DIM) ** -0.5)
    qspec = pl.BlockSpec((None, tile, LANES), lambda bi, h, i: (bi, i, h))
    rspec = pl.BlockSpec((None, tile, LANES), lambda bi, h, i: (bi, i, h // ROPE_GROUP))
    return pl.pallas_call(
        kern,
        grid=(b, heads, s // tile),
        in_specs=[
            qspec, rspec, rspec,
            pl.BlockSpec((None, s, LANES), lambda bi, h, i: (bi, 0, 2 * h)),
            pl.BlockSpec((None, s, LANES), lambda bi, h, i: (bi, 0, 2 * h + 1)),
            pl.BlockSpec((None, s, LANES), lambda bi, h, i: (bi, 0, 0)),
            pl.BlockSpec((None, s, LANES), lambda bi, h, i: (bi, 0, 0)),
        ],
        out_specs=qspec,
        out_shape=jax.ShapeDtypeStruct((b, s, heads * HEAD_DIM), BF16),
        compiler_params=_cparams("parallel", "parallel", "arbitrary"),
        name="mla_attention",
    )(qn, qr1, qr2, kv, kv, kr1, kr2)


def _topk_rows(s, payload, k):
    rows = s.shape[0]
    iota = lax.broadcasted_iota(jnp.int32, s.shape, 0).astype(F32)
    vals, picked = [], []
    for _ in range(k):
        m = jnp.max(s, axis=0, keepdims=True)
        first = jnp.min(jnp.where(s == m, iota, float(rows)), axis=0, keepdims=True)
        hit = iota == first
        picked.append(jnp.max(jnp.where(hit, payload, -1.0), axis=0, keepdims=True))
        vals.append(m)
        s = jnp.where(hit, -jnp.inf, s)
    return jnp.concatenate(vals, axis=0), jnp.concatenate(picked, axis=0)


def _peer_route_kernel(q_ref, k1_ref, k2_ref, ids_ref, gate_ref, *, chunks, n_experts):
    def chunk(c, carry):
        qc = q_ref[pl.ds(pl.multiple_of(c * LANES, LANES), LANES), :]
        s1 = _dot_nt(k1_ref[...], qc[:, :PEER_HALF])
        s2 = _dot_nt(k2_ref[...], qc[:, PEER_HALF:])
        key_id = lax.broadcasted_iota(jnp.int32, s1.shape, 0).astype(F32)
        v1, i1 = _topk_rows(s1, key_id, PEER_TOPK)
        v2, i2 = _topk_rows(s2, key_id, PEER_TOPK)
        cand = jnp.concatenate([v1[a:a + 1] + v2 for a in range(PEER_TOPK)], axis=0)
        cand_id = jnp.concatenate([i1[a:a + 1] * float(PEER_KEYS) + i2 for a in range(PEER_TOPK)], axis=0)
        top, ids = _topk_rows(cand, cand_id, PEER_TOPK)
        e = jnp.exp(top - top[0:1])
        gate_ref[c] = e / jnp.sum(e, axis=0, keepdims=True)
        ids_ref[c] = jnp.clip(ids.astype(jnp.int32), 0, n_experts - 1)
        return carry

    lax.fori_loop(0, chunks, chunk, 0)


def _peer_route(q, keys_1, keys_2, n_experts):
    n = q.shape[0]
    heads = keys_1.shape[0]
    tm = min(ROUTE_TILE, n)
    chunks = tm // LANES
    kern = functools.partial(_peer_route_kernel, chunks=chunks, n_experts=n_experts)
    ospec = pl.BlockSpec((chunks, PEER_TOPK, LANES), lambda i, h: (i, h, 0))
    kspec = pl.BlockSpec((None, PEER_KEYS, PEER_HALF), lambda i, h: (h, 0, 0))
    return pl.pallas_call(
        kern,
        grid=(n // tm, heads),
        in_specs=[pl.BlockSpec((tm, 2 * PEER_HALF), lambda i, h: (i, h)), kspec, kspec],
        out_specs=[ospec, ospec],
        out_shape=[
            jax.ShapeDtypeStruct((n // LANES, heads * PEER_TOPK, LANES), jnp.int32),
            jax.ShapeDtypeStruct((n // LANES, heads * PEER_TOPK, LANES), F32),
        ],
        compiler_params=_cparams("parallel", "parallel"),
        name="peer_route",
    )(q, keys_1.astype(BF16), keys_2.astype(BF16))


def _peer_expert_kernel(ids_cur_ref, ids_nxt_ref, x_ref, g_ref, gate_ref, uv_hbm, o_ref,
                        uvbuf, wbuf, sem, *, tokens, n_sel, slabs, d_model):
    i = pl.program_id(0)
    n_steps = pl.num_programs(0)
    slot = i % 2

    def row_copy(ids_ref, to_slot, t, e):
        return pltpu.make_async_copy(
            uv_hbm.at[ids_ref[0, t, e]], uvbuf.at[to_slot, t * n_sel + e], sem.at[to_slot, t])

    def fetch(ids_ref, to_slot):
        def body(e, carry):
            for t in range(tokens):
                row_copy(ids_ref, to_slot, t, e).start()
            return carry
        lax.fori_loop(0, n_sel, body, 0)

    @pl.when(i == 0)
    def _():
        fetch(ids_cur_ref, 0)

    @pl.when(i + 1 < n_steps)
    def _():
        fetch(ids_nxt_ref, 1 - slot)

    eye = (lax.broadcasted_iota(jnp.int32, (n_sel, n_sel), 0)
           == lax.broadcasted_iota(jnp.int32, (n_sel, n_sel), 1))
    norm_g = g_ref[...]

    def token(t, carry):
        base = pl.multiple_of(t * n_sel, n_sel)
        rows = uvbuf.at[slot, pl.ds(base, n_sel)]
        pltpu.make_async_copy(uv_hbm.at[pl.ds(0, n_sel)], rows, sem.at[slot, t]).wait()
        xt = x_ref[t]
        ms = jnp.sum(jnp.sum(xt * xt, axis=0, keepdims=True), axis=1, keepdims=True) / d_model
        ht = xt * lax.rsqrt(ms + EPS) * norm_g
        parts = []
        for e0 in range(0, n_sel, 8):
            u = rows[e0:e0 + 8, 0:slabs, :]
            parts.append(jnp.sum(u * ht[None], axis=1))
        part = jnp.concatenate(parts, axis=0)
        s = jnp.sum(part, axis=-1, keepdims=True)
        act = 0.5 * s * (1.0 + lax.erf(s * (2.0 ** -0.5)))
        gate_row = gate_ref[pl.ds(t, 1), :]
        gate_col = jnp.sum(jnp.where(eye, gate_row, 0.0), axis=-1, keepdims=True)
        wbuf[...] = jnp.broadcast_to(act * gate_col, (n_sel, LANES))
        y = xt
        for e in range(n_sel):
            y = y + wbuf[e:e + 1, :] * rows[e, slabs:2 * slabs, :]
        o_ref[t] = y
        return carry

    lax.fori_loop(0, tokens, token, 0)


def _peer_experts(x, norm_g, ids, gates, uv):
    n, d = x.shape
    n_sel = ids.shape[1]
    slabs = d // LANES
    tt = min(GATHER_TOKENS, n)
    steps = n // tt
    kern = functools.partial(_peer_expert_kernel, tokens=tt, n_sel=n_sel, slabs=slabs, d_model=d)
    ids3 = ids.reshape(steps, tt, n_sel)
    out = pl.pallas_call(
        kern,
        grid=(steps,),
        in_specs=[
            pl.BlockSpec((1, tt, n_sel), lambda i: (i, 0, 0), memory_space=pltpu.SMEM),
            pl.BlockSpec((1, tt, n_sel), lambda i: (jnp.minimum(i + 1, steps - 1), 0, 0),
                         memory_space=pltpu.SMEM),
            pl.BlockSpec((tt, slabs, LANES), lambda i: (i, 0, 0)),
            pl.BlockSpec((slabs, LANES), lambda i: (0, 0)),
            pl.BlockSpec((tt, n_sel), lambda i: (i, 0)),
            pl.BlockSpec(memory_space=pl.ANY),
        ],
        out_specs=pl.BlockSpec((tt, slabs, LANES), lambda i: (i, 0, 0)),
        out_shape=jax.ShapeDtypeStruct((n, slabs, LANES), F32),
        scratch_shapes=[
            pltpu.VMEM((2, tt * n_sel, 2 * slabs, LANES), F32),
            pltpu.VMEM((n_sel, LANES), F32),
            pltpu.SemaphoreType.DMA((2, tt)),
        ],
        compiler_params=_cparams("arbitrary"),
        name="peer_experts",
    )(ids3, ids3, x.reshape(n, slabs, LANES), norm_g.reshape(slabs, LANES), gates, uv)
    return out.reshape(n, d)


def _peer_ffn(x, norm_g, w_q, keys_1, keys_2, table_u, table_v):
    n, d = x.shape
    n_experts = table_u.shape[0]
    q = _norm_matmul(x, norm_g, w_q, BF16)
    ids, gates = _peer_route(q, keys_1, keys_2, n_experts)
    ids = jnp.swapaxes(ids, 1, 2).reshape(n, -1)
    gates = jnp.swapaxes(gates, 1, 2).reshape(n, -1)
    slabs = d // LANES
    uv = jnp.concatenate([table_u.reshape(n_experts, slabs, LANES),
                          table_v.reshape(n_experts, slabs, LANES)], axis=1)
    return _peer_experts(x, norm_g, ids, gates, uv)


def _tile_cols(w, reps):
    return jnp.concatenate([w] * reps, axis=1)


def kernel(x, positions, sb_norm_g, sb_w_qkv, sb_w_o, kv_norm_g, mla_w_dkv, mla_kv_latent_g,
           mla_w_ukv, mla_norm_g, mla_w_dq, mla_q_latent_g, mla_w_uq, mla_w_o, peer_norm_g,
           peer_w_q, peer_keys_1, peer_keys_2, peer_u, peer_v, final_norm_g):
    b, s, d = x.shape
    n = b * s
    depth = peer_norm_g.shape[0]
    n_a = sb_norm_g.shape[0]
    xf = x.reshape(n, d)
    pos = positions.reshape(n, 1).astype(F32)
    inv_freq = ROPE_BASE ** (-jnp.arange(0, ROPE_DIM, 2, dtype=F32) / ROPE_DIM)
    freq = jnp.tile(inv_freq, ROPE_GROUP).reshape(1, LANES)

    kv = kr1 = kr2 = None
    for layer in range(depth):
        if layer < n_a:
            heads = sb_w_qkv.shape[2] // (3 * HEAD_DIM)
            qkv = _norm_matmul(xf, sb_norm_g[layer], sb_w_qkv[layer], BF16)
            o = _sb_attention(qkv.reshape(b, s, -1), heads)
            xf = _matmul_residual(o.reshape(n, -1), sb_w_o[layer], xf)
        else:
            kv_lora = mla_kv_latent_g.shape[0]
            heads = mla_w_ukv.shape[1] // (2 * HEAD_DIM)
            if layer == n_a:
                w_dkv = jnp.concatenate([
                    mla_w_dkv[:, :kv_lora],
                    _tile_cols(mla_w_dkv[:, kv_lora:kv_lora + ROPE_HALF], ROPE_GROUP),
                    _tile_cols(mla_w_dkv[:, kv_lora + ROPE_HALF:], ROPE_GROUP)], axis=1)
                ckv = _norm_matmul(xf, kv_norm_g, w_dkv, F32, tn=LANES * 2)
                kv = _norm_matmul(ckv, mla_kv_latent_g, mla_w_ukv, BF16, k=kv_lora).reshape(b, s, -1)
                blk = kv_lora // LANES
                kr1, kr2 = _rope(ckv, blk, blk + 1, 1, pos, freq)
                kr1 = kr1.reshape(b, s, LANES)
                kr2 = kr2.reshape(b, s, LANES)
            j = layer - n_a
            q_lora = mla_q_latent_g.shape[1]
            cq = _norm_matmul(xf, mla_norm_g[j], mla_w_dq[j], F32)
            w_uq = mla_w_uq[j].reshape(q_lora, heads, HEAD_DIM + ROPE_DIM)
            w_all = jnp.concatenate([
                w_uq[:, :, :HEAD_DIM].reshape(q_lora, -1),
                w_uq[:, :, HEAD_DIM:HEAD_DIM + ROPE_HALF].reshape(q_lora, -1),
                w_uq[:, :, HEAD_DIM + ROPE_HALF:].reshape(q_lora, -1)], axis=1)
            qall = _norm_matmul(cq, mla_q_latent_g[j], w_all, BF16)
            nope_blk = heads * HEAD_DIM // LANES
            rope_blk = heads * ROPE_HALF // LANES
            qr1, qr2 = _rope(qall, nope_blk, nope_blk + rope_blk, rope_blk, pos, freq)
            o = _mla_attention(qall.reshape(b, s, -1), qr1.reshape(b, s, -1), qr2.reshape(b, s, -1),
                               kv, kr1, kr2, heads)
            xf = _matmul_residual(o.reshape(n, -1), mla_w_o[j], xf)
        xf = _peer_ffn(xf, peer_norm_g[layer], peer_w_q[layer], peer_keys_1[layer],
                       peer_keys_2[layer], peer_u[layer], peer_v[layer])
    return _rmsnorm(xf, final_norm_g).reshape(b, s, d)
```

```python
import functools
import math

import jax
import jax.numpy as jnp
from jax import lax
from jax.experimental import pallas as pl
from jax.experimental.pallas import tpu as pltpu

F32 = jnp.float32
BF16 = jnp.bfloat16

EPS = 1e-6
F32_EXP_UNDERFLOW = -104.0
LANES = 128
HEAD_DIM = 128
ROPE_DIM = 64
ROPE_HALF = ROPE_DIM // 2
ROPE_BASE = 10000.0
ROPE_GROUP = LANES // ROPE_HALF
PEER_TOPK = 16
PEER_KEYS = 128
PEER_HALF = 128

ATTN_TILE = 256
HEAD_GROUP = 4
ROW_TILE = 512
ROUTE_TILE = 512
GATHER_TOKENS = 8
VMEM_LIMIT = 56 * 1024 * 1024


def _cparams(*sem):
    return pltpu.CompilerParams(dimension_semantics=sem, vmem_limit_bytes=VMEM_LIMIT)


def _fit_tile(size, tile):
    tile = min(tile, size)
    while size % tile:
        tile //= 2
    return tile


def _dot(a, b):
    return jnp.dot(a, b, preferred_element_type=F32)


def _dot_nt(a, b):
    return lax.dot_general(a, b, (((1,), (1,)), ((), ())), preferred_element_type=F32)


def _norm_matmul_kernel(x_ref, g_ref, w_ref, o_ref, xn_ref):
    @pl.when(pl.program_id(1) == 0)
    def _():
        x = x_ref[...]
        ms = jnp.mean(x * x, axis=-1, keepdims=True)
        xn_ref[...] = (x * lax.rsqrt(ms + EPS) * g_ref[...]).astype(BF16)

    o_ref[...] = _dot(xn_ref[...], w_ref[...]).astype(o_ref.dtype)


def _norm_matmul(x, g, w, out_dtype, *, k=None, tn=512):
    n = x.shape[0]
    k = x.shape[1] if k is None else k
    f = w.shape[1]
    tm = _fit_tile(n, ROW_TILE)
    tn = _fit_tile(f, tn)
    assert w.shape[0] == k
    return pl.pallas_call(
        _norm_matmul_kernel,
        grid=(n // tm, f // tn),
        in_specs=[
            pl.BlockSpec((tm, k), lambda i, j: (i, 0)),
            pl.BlockSpec((1, k), lambda i, j: (0, 0)),
            pl.BlockSpec((k, tn), lambda i, j: (0, j)),
        ],
        out_specs=pl.BlockSpec((tm, tn), lambda i, j: (i, j)),
        out_shape=jax.ShapeDtypeStruct((n, f), out_dtype),
        scratch_shapes=[pltpu.VMEM((tm, k), BF16)],
        compiler_params=_cparams("parallel", "arbitrary"),
        name="norm_matmul",
    )(x, g.reshape(1, k).astype(F32), w.astype(BF16))


def _matmul_residual_kernel(a_ref, w_ref, x_ref, o_ref):
    o_ref[...] = x_ref[...] + _dot(a_ref[...], w_ref[...])


def _matmul_residual(a, w, x, *, tn=512):
    n, k = a.shape
    f = w.shape[1]
    tm = _fit_tile(n, ROW_TILE)
    tn = _fit_tile(f, tn)
    return pl.pallas_call(
        _matmul_residual_kernel,
        grid=(n // tm, f // tn),
        in_specs=[
            pl.BlockSpec((tm, k), lambda i, j: (i, 0)),
            pl.BlockSpec((k, tn), lambda i, j: (0, j)),
            pl.BlockSpec((tm, tn), lambda i, j: (i, j)),
        ],
        out_specs=pl.BlockSpec((tm, tn), lambda i, j: (i, j)),
        out_shape=jax.ShapeDtypeStruct((n, f), F32),
        compiler_params=_cparams("parallel", "arbitrary"),
        name="matmul_residual",
    )(a, w.astype(BF16), x)


def _rmsnorm_kernel(x_ref, g_ref, o_ref):
    x = x_ref[...]
    ms = jnp.mean(x * x, axis=-1, keepdims=True)
    o_ref[...] = x * lax.rsqrt(ms + EPS) * g_ref[...]


def _rmsnorm(x, g):
    n, d = x.shape
    tm = min(ROW_TILE, n)
    return pl.pallas_call(
        _rmsnorm_kernel,
        grid=(n // tm,),
        in_specs=[pl.BlockSpec((tm, d), lambda i: (i, 0)), pl.BlockSpec((1, d), lambda i: (0, 0))],
        out_specs=pl.BlockSpec((tm, d), lambda i: (i, 0)),
        out_shape=jax.ShapeDtypeStruct((n, d), F32),
        compiler_params=_cparams("parallel"),
        name="final_rmsnorm",
    )(x, g.reshape(1, d))


def _sb_attn_kernel(q_ref, k_ref, v_ref, o_ref, acc_ref, carry_ref, *, tile, scale, group):
    i = pl.program_id(2)
    row = lax.broadcasted_iota(jnp.int32, (tile, tile), 0)
    col = lax.broadcasted_iota(jnp.int32, (tile, tile), 1)
    later = (row > col).astype(BF16)
    causal = col < row

    def visit(j, masked):
        rows = pl.ds(pl.multiple_of(j * tile, tile), tile)
        lanes = [slice(g * HEAD_DIM, (g + 1) * HEAD_DIM) for g in range(group)]
        zs = [_dot_nt(q_ref[:, lanes[g]], k_ref[rows, lanes[g]]) * scale for g in range(group)]
        log_1m_betas, log_betas = [], []
        for z in zs:
            softplus = jnp.maximum(z, 0.0) + jnp.log(1.0 + jnp.exp(-jnp.abs(z)))
            log_1m_beta = -softplus
            if masked:
                log_1m_beta = jnp.where(causal, log_1m_beta, 0.0)
            log_1m_betas.append(log_1m_beta)
            log_betas.append(z - softplus)
        tails = []
        for log_1m_beta in log_1m_betas:
            hi = log_1m_beta.astype(BF16)
            lo = (log_1m_beta - hi.astype(F32)).astype(BF16)
            tails.append(_dot(hi, later) + _dot(lo, later))
        carries = [carry_ref[g] for g in range(group)]
        weights = []
        for g in range(group):
            a = jnp.exp(log_betas[g] + tails[g] + carries[g])
            if masked:
                a = jnp.where(causal, a, 0.0)
            weights.append(a.astype(BF16))
        outs = [_dot(weights[g], v_ref[rows, lanes[g]]) for g in range(group)]
        for g in range(group):
            acc_ref[:, lanes[g]] += outs[g]
            carry_ref[g] = carries[g] + tails[g][:, 0:1] + log_1m_betas[g][:, 0:1]

    acc_ref[...] = jnp.zeros_like(acc_ref)
    carry_ref[...] = jnp.zeros_like(carry_ref)
    visit(i, True)

    def any_row_alive():
        return jnp.max(carry_ref[...]) > F32_EXP_UNDERFLOW

    def cond(state):
        return jnp.logical_and(state[0] < i, state[1])

    def body(state):
        visit(i - 1 - state[0], False)
        return state[0] + 1, any_row_alive()

    lax.while_loop(cond, body, (0, any_row_alive()))
    o_ref[...] = acc_ref[...].astype(o_ref.dtype)


def _sb_attention(qkv, heads):
    b, s, _ = qkv.shape
    tile = _fit_tile(s, ATTN_TILE)
    group = _fit_tile(heads, HEAD_GROUP)
    groups = heads // group
    width = group * HEAD_DIM
    kern = functools.partial(_sb_attn_kernel, tile=tile, scale=HEAD_DIM ** -0.5, group=group)
    return pl.pallas_call(
        kern,
        grid=(b, groups, s // tile),
        in_specs=[
            pl.BlockSpec((None, tile, width), lambda bi, g, i: (bi, i, g)),
            pl.BlockSpec((None, s, width), lambda bi, g, i: (bi, 0, groups + g)),
            pl.BlockSpec((None, s, width), lambda bi, g, i: (bi, 0, 2 * groups + g)),
        ],
        out_specs=pl.BlockSpec((None, tile, width), lambda bi, g, i: (bi, i, g)),
        out_shape=jax.ShapeDtypeStruct((b, s, heads * HEAD_DIM), BF16),
        scratch_shapes=[pltpu.VMEM((tile, width), F32), pltpu.VMEM((group, tile, 1), F32)],
        compiler_params=_cparams("parallel", "parallel", "arbitrary"),
        name="sb_attention",
    )(qkv, qkv, qkv)


def _rope_kernel(x1_ref, x2_ref, pos_ref, freq_ref, o1_ref, o2_ref):
    ang = pos_ref[...] * freq_ref[...]
    c = jnp.cos(ang)
    s = jnp.sin(ang)
    x1 = x1_ref[...].astype(F32)
    x2 = x2_ref[...].astype(F32)
    o1_ref[...] = (x1 * c - x2 * s).astype(o1_ref.dtype)
    o2_ref[...] = (x2 * c + x1 * s).astype(o2_ref.dtype)


def _rope(src, col1, col2, nblk, pos, freq):
    n = src.shape[0]
    tm = min(ROW_TILE, n)
    out = jax.ShapeDtypeStruct((n, nblk * LANES), BF16)
    return pl.pallas_call(
        _rope_kernel,
        grid=(n // tm, nblk),
        in_specs=[
            pl.BlockSpec((tm, LANES), lambda i, j: (i, col1 + j)),
            pl.BlockSpec((tm, LANES), lambda i, j: (i, col2 + j)),
            pl.BlockSpec((tm, 1), lambda i, j: (i, 0)),
            pl.BlockSpec((1, LANES), lambda i, j: (0, 0)),
        ],
        out_specs=[pl.BlockSpec((tm, LANES), lambda i, j: (i, j))] * 2,
        out_shape=[out, out],
        compiler_params=_cparams("parallel", "parallel"),
        name="rope",
    )(src, src, pos, freq)


def _mla_attn_kernel(qn_ref, qr1_ref, qr2_ref, kv_ref, kr1_ref, kr2_ref, o_ref,
                     acc_ref, m_ref, l_ref, qr_ref, *, tile, scale):
    i = pl.program_id(2)
    lane = lax.broadcasted_iota(jnp.int32, (1, LANES), 1)
    row = lax.broadcasted_iota(jnp.int32, (tile, tile), 0)
    col = lax.broadcasted_iota(jnp.int32, (tile, tile), 1)
    causal = col <= row
    zero = jnp.zeros((), BF16)

    def visit(j, masked):
        rows = pl.ds(pl.multiple_of(j * tile, tile), tile)
        kr1 = kr1_ref[rows, :]
        kr2 = kr2_ref[rows, :]
        group = range(ROPE_GROUP)
        lanes = [slice(g * HEAD_DIM, (g + 1) * HEAD_DIM) for g in group]
        k_lanes = [slice(2 * g * HEAD_DIM, (2 * g + 1) * HEAD_DIM) for g in group]
        v_lanes = [slice((2 * g + 1) * HEAD_DIM, (2 * g + 2) * HEAD_DIM) for g in group]
        scores = []
        for g in group:
            sc = (_dot_nt(qn_ref[:, lanes[g]], kv_ref[rows, k_lanes[g]])
                  + _dot_nt(qr_ref[g, :, :LANES], kr1) + _dot_nt(qr_ref[g, :, LANES:], kr2)) * scale
            if masked:
                sc = jnp.where(causal, sc, -jnp.inf)
            scores.append(sc)
        m_old = [m_ref[g] for g in group]
        m_new = [jnp.maximum(m_old[g], jnp.max(scores[g], axis=-1, keepdims=True)) for g in group]
        probs = [jnp.exp(scores[g] - m_new[g]) for g in group]
        outs = [_dot(probs[g].astype(BF16), kv_ref[rows, v_lanes[g]]) for g in group]
        for g in group:
            alpha = jnp.exp(m_old[g] - m_new[g])
            l_ref[g] = alpha * l_ref[g] + jnp.sum(probs[g], axis=-1, keepdims=True)
            acc_ref[:, lanes[g]] = alpha * acc_ref[:, lanes[g]] + outs[g]
            m_ref[g] = m_new[g]

    for g in range(ROPE_GROUP):
        mine = (lane // ROPE_HALF) == g
        qr_ref[g] = jnp.concatenate([jnp.where(mine, qr1_ref[...], zero),
                                     jnp.where(mine, qr2_ref[...], zero)], axis=1)
    acc_ref[...] = jnp.zeros_like(acc_ref)
    l_ref[...] = jnp.zeros_like(l_ref)
    m_ref[...] = jnp.full_like(m_ref, -jnp.inf)
    visit(i, True)

    def body(n, c):
        visit(n, False)
        return c

    lax.fori_loop(0, i, body, 0)
    for g in range(ROPE_GROUP):
        lanes = slice(g * HEAD_DIM, (g + 1) * HEAD_DIM)
        o_ref[:, lanes] = (acc_ref[:, lanes] / l_ref[g]).astype(o_ref.dtype)


def _mla_attention(qn, qr1, qr2, kv, kr1, kr2, heads):
    b, s, _ = qn.shape
    tile = _fit_tile(s, ATTN_TILE)
    assert heads % ROPE_GROUP == 0
    width = ROPE_GROUP * HEAD_DIM
    kern = functools.partial(_mla_attn_kernel, tile=tile, scale=(HEAD_DIM + ROPE_DIM) ** -0.5)
    qspec = pl.BlockSpec((None, tile, width), lambda bi, g, i: (bi, i, g))
    rspec = pl.BlockSpec((None, tile, LANES), lambda bi, g, i: (bi, i, g))
    shared = pl.BlockSpec((None, s, LANES), lambda bi, g, i: (bi, 0, 0))
    return pl.pallas_call(
        kern,
        grid=(b, heads // ROPE_GROUP, s // tile),
        in_specs=[
            qspec, rspec, rspec,
            pl.BlockSpec((None, s, 2 * width), lambda bi, g, i: (bi, 0, g)),
            shared, shared,
        ],
        out_specs=qspec,
        out_shape=jax.ShapeDtypeStruct((b, s, heads * HEAD_DIM), BF16),
        scratch_shapes=[pltpu.VMEM((tile, width), F32),
                        pltpu.VMEM((ROPE_GROUP, tile, 1), F32),
                        pltpu.VMEM((ROPE_GROUP, tile, 1), F32),
                        pltpu.VMEM((ROPE_GROUP, tile, 2 * LANES), BF16)],
        compiler_params=_cparams("parallel", "parallel", "arbitrary"),
        name="mla_attention",
    )(qn, qr1, qr2, kv, kr1, kr2)


def _topk_rows(s, k, payload=None):
    rows = s.shape[0]
    iota = lax.broadcasted_iota(jnp.int32, s.shape, 0).astype(F32)
    vals, picked = [], []
    for _ in range(k):
        m = jnp.max(s, axis=0, keepdims=True)
        first = jnp.min(jnp.where(s == m, iota, float(rows)), axis=0, keepdims=True)
        hit = iota == first
        if payload is None:
            picked.append(first)
        else:
            picked.append(jnp.max(jnp.where(hit, payload, -1.0), axis=0, keepdims=True))
        vals.append(m)
        s = jnp.where(hit, -jnp.inf, s)
    return jnp.concatenate(vals, axis=0), jnp.concatenate(picked, axis=0)


def _pruned_candidates(v1, i1, v2, i2):
    assert PEER_TOPK == 16
    sub = lax.broadcasted_iota(jnp.int32, (8, v2.shape[1]), 0)
    scores = [v1[0:1] + v2]
    ids = [i1[0:1] * float(PEER_KEYS) + i2]
    for a in range(1, 8):
        keep = PEER_TOPK // (a + 1)
        blk = v1[a:a + 1] + v2[0:8]
        if keep < 8:
            blk = jnp.where(sub < keep, blk, -jnp.inf)
        scores.append(blk)
        ids.append(i1[a:a + 1] * float(PEER_KEYS) + i2[0:8])
    scores.append(v1[8:16] + v2[0:1])
    ids.append(i1[8:16] * float(PEER_KEYS) + i2[0:1])
    return jnp.concatenate(scores, axis=0), jnp.concatenate(ids, axis=0)


def _peer_route_kernel(q_ref, k1_ref, k2_ref, ids_ref, gate_ref, *, chunks, n_experts):
    def chunk(c, carry):
        qc = q_ref[pl.ds(pl.multiple_of(c * LANES, LANES), LANES), :]
        s1 = _dot_nt(k1_ref[...], qc[:, :PEER_HALF])
        s2 = _dot_nt(k2_ref[...], qc[:, PEER_HALF:])
        v1, i1 = _topk_rows(s1, PEER_TOPK)
        v2, i2 = _topk_rows(s2, PEER_TOPK)
        cand, cand_id = _pruned_candidates(v1, i1, v2, i2)
        top, ids = _topk_rows(cand, PEER_TOPK, cand_id)
        e = jnp.exp(top - top[0:1])
        gate_ref[c] = e / jnp.sum(e, axis=0, keepdims=True)
        ids_ref[c] = jnp.clip(ids.astype(jnp.int32), 0, n_experts - 1)
        return carry

    lax.fori_loop(0, chunks, chunk, 0)


def _peer_route(q, keys_1, keys_2, n_experts):
    n = q.shape[0]
    heads = keys_1.shape[0]
    tm = min(ROUTE_TILE, n)
    chunks = tm // LANES
    kern = functools.partial(_peer_route_kernel, chunks=chunks, n_experts=n_experts)
    ospec = pl.BlockSpec((chunks, PEER_TOPK, LANES), lambda i, h: (i, h, 0))
    kspec = pl.BlockSpec((None, PEER_KEYS, PEER_HALF), lambda i, h: (h, 0, 0))
    return pl.pallas_call(
        kern,
        grid=(n // tm, heads),
        in_specs=[pl.BlockSpec((tm, 2 * PEER_HALF), lambda i, h: (i, h)), kspec, kspec],
        out_specs=[ospec, ospec],
        out_shape=[
            jax.ShapeDtypeStruct((n // LANES, heads * PEER_TOPK, LANES), jnp.int32),
            jax.ShapeDtypeStruct((n // LANES, heads * PEER_TOPK, LANES), F32),
        ],
        compiler_params=_cparams("parallel", "parallel"),
        name="peer_route",
    )(q, keys_1.astype(BF16), keys_2.astype(BF16))


def _peer_expert_kernel(ids_cur_ref, ids_nxt_ref, x_ref, g_ref, gate_ref, uv_hbm, o_ref,
                        uvbuf, wbuf, sem, *, tokens, n_sel, slabs, d_model):
    i = pl.program_id(0)
    n_steps = pl.num_programs(0)
    slot = i % 2
    nxt = 1 - slot

    def row_copy(ids_ref, to_slot, t, e):
        return pltpu.make_async_copy(
            uv_hbm.at[ids_ref[0, t, e]], uvbuf.at[to_slot, t * n_sel + e], sem.at[to_slot, t])

    def token_rows_wait(in_slot, t):
        rows = uvbuf.at[in_slot, pl.ds(pl.multiple_of(t * n_sel, n_sel), n_sel)]
        pltpu.make_async_copy(uv_hbm.at[pl.ds(0, n_sel)], rows, sem.at[in_slot, t]).wait()
        return rows

    @pl.when(i == 0)
    def _():
        def body(e, carry):
            for t in range(tokens):
                row_copy(ids_cur_ref, 0, t, e).start()
            return carry
        lax.fori_loop(0, n_sel, body, 0)

    eye = (lax.broadcasted_iota(jnp.int32, (n_sel, n_sel), 0)
           == lax.broadcasted_iota(jnp.int32, (n_sel, n_sel), 1))
    norm_g = g_ref[...]

    def token(t, carry):
        rows = token_rows_wait(slot, t)
        xt = x_ref[t]
        ms = jnp.sum(jnp.sum(xt * xt, axis=0, keepdims=True), axis=1, keepdims=True) / d_model
        ht = xt * lax.rsqrt(ms + EPS) * norm_g
        parts = []
        for e0 in range(0, n_sel, 8):
            for e in range(e0, e0 + 8):
                row_copy(ids_nxt_ref, nxt, t, e).start()
            u = rows[e0:e0 + 8, 0:slabs, :]
            parts.append(jnp.sum(u * ht[None], axis=1))
        part = jnp.concatenate(parts, axis=0)
        s = jnp.sum(part, axis=-1, keepdims=True)
        act = 0.5 * s * (1.0 + lax.erf(s * (2.0 ** -0.5)))
        gate_row = gate_ref[pl.ds(t, 1), :]
        gate_col = jnp.sum(jnp.where(eye, gate_row, 0.0), axis=-1, keepdims=True)
        wbuf[...] = jnp.broadcast_to(act * gate_col, (n_sel, LANES))
        y = xt
        for e in range(n_sel):
            y = y + wbuf[e:e + 1, :] * rows[e, slabs:2 * slabs, :]
        o_ref[t] = y
        return carry

    lax.fori_loop(0, tokens, token, 0)

    @pl.when(i == n_steps - 1)
    def _():
        for t in range(tokens):
            token_rows_wait(nxt, t)


def _peer_experts(x, norm_g, ids, gates, uv):
    n, d = x.shape
    n_sel = ids.shape[1]
    slabs = d // LANES
    tt = min(GATHER_TOKENS, n)
    steps = n // tt
    kern = functools.partial(_peer_expert_kernel, tokens=tt, n_sel=n_sel, slabs=slabs, d_model=d)
    ids3 = ids.reshape(steps, tt, n_sel)
    out = pl.pallas_call(
        kern,
        grid=(steps,),
        in_specs=[
            pl.BlockSpec((1, tt, n_sel), lambda i: (i, 0, 0), memory_space=pltpu.SMEM),
            pl.BlockSpec((1, tt, n_sel), lambda i: (jnp.minimum(i + 1, steps - 1), 0, 0),
                         memory_space=pltpu.SMEM),
            pl.BlockSpec((tt, slabs, LANES), lambda i: (i, 0, 0)),
            pl.BlockSpec((slabs, LANES), lambda i: (0, 0)),
            pl.BlockSpec((tt, n_sel), lambda i: (i, 0)),
            pl.BlockSpec(memory_space=pl.ANY),
        ],
        out_specs=pl.BlockSpec((tt, slabs, LANES), lambda i: (i, 0, 0)),
        out_shape=jax.ShapeDtypeStruct((n, slabs, LANES), F32),
        scratch_shapes=[
            pltpu.VMEM((2, tt * n_sel, 2 * slabs, LANES), F32),
            pltpu.VMEM((n_sel, LANES), F32),
            pltpu.SemaphoreType.DMA((2, tt)),
        ],
        compiler_params=_cparams("arbitrary"),
        name="peer_experts",
    )(ids3, ids3, x.reshape(n, slabs, LANES), norm_g.reshape(slabs, LANES), gates, uv)
    return out.reshape(n, d)


def _peer_ffn(x, norm_g, w_q, keys_1, keys_2, table_u, table_v):
    n, d = x.shape
    n_experts = table_u.shape[0]
    q = _norm_matmul(x, norm_g, w_q, BF16)
    ids, gates = _peer_route(q, keys_1, keys_2, n_experts)
    ids = jnp.swapaxes(ids, 1, 2).reshape(n, -1)
    gates = jnp.swapaxes(gates, 1, 2).reshape(n, -1)
    slabs = d // LANES
    uv = jnp.concatenate([table_u.reshape(n_experts, slabs, LANES),
                          table_v.reshape(n_experts, slabs, LANES)], axis=1)
    return _peer_experts(x, norm_g, ids, gates, uv)


def _tile_cols(w, reps):
    return jnp.concatenate([w] * reps, axis=1)


def kernel(x, positions, sb_norm_g, sb_w_qkv, sb_w_o, kv_norm_g, mla_w_dkv, mla_kv_latent_g,
           mla_w_ukv, mla_norm_g, mla_w_dq, mla_q_latent_g, mla_w_uq, mla_w_o, peer_norm_g,
           peer_w_q, peer_keys_1, peer_keys_2, peer_u, peer_v, final_norm_g):
    b, s, d = x.shape
    n = b * s
    depth = peer_norm_g.shape[0]
    n_a = sb_norm_g.shape[0]
    xf = x.reshape(n, d)
    pos = positions.reshape(n, 1).astype(F32)
    inv_freq = ROPE_BASE ** (-jnp.arange(0, ROPE_DIM, 2, dtype=F32) / ROPE_DIM)
    freq = jnp.tile(inv_freq, ROPE_GROUP).reshape(1, LANES)

    kv = kr1 = kr2 = None
    for layer in range(depth):
        if layer < n_a:
            heads = sb_w_qkv.shape[2] // (3 * HEAD_DIM)
            qkv = _norm_matmul(xf, sb_norm_g[layer], sb_w_qkv[layer], BF16)
            o = _sb_attention(qkv.reshape(b, s, -1), heads)
            xf = _matmul_residual(o.reshape(n, -1), sb_w_o[layer], xf)
        else:
            kv_lora = mla_kv_latent_g.shape[0]
            heads = mla_w_ukv.shape[1] // (2 * HEAD_DIM)
            if layer == n_a:
                w_dkv = jnp.concatenate([
                    mla_w_dkv[:, :kv_lora],
                    _tile_cols(mla_w_dkv[:, kv_lora:kv_lora + ROPE_HALF], ROPE_GROUP),
                    _tile_cols(mla_w_dkv[:, kv_lora + ROPE_HALF:], ROPE_GROUP)], axis=1)
                ckv = _norm_matmul(xf, kv_norm_g, w_dkv, F32, tn=LANES * 2)
                kv = _norm_matmul(ckv, mla_kv_latent_g, mla_w_ukv, BF16, k=kv_lora).reshape(b, s, -1)
                blk = kv_lora // LANES
                kr1, kr2 = _rope(ckv, blk, blk + 1, 1, pos, freq)
                kr1 = kr1.reshape(b, s, LANES)
                kr2 = kr2.reshape(b, s, LANES)
            j = layer - n_a
            q_lora = mla_q_latent_g.shape[1]
            cq = _norm_matmul(xf, mla_norm_g[j], mla_w_dq[j], F32)
            w_uq = mla_w_uq[j].reshape(q_lora, heads, HEAD_DIM + ROPE_DIM)
            w_all = jnp.concatenate([
                w_uq[:, :, :HEAD_DIM].reshape(q_lora, -1),
                w_uq[:, :, HEAD_DIM:HEAD_DIM + ROPE_HALF].reshape(q_lora, -1),
                w_uq[:, :, HEAD_DIM + ROPE_HALF:].reshape(q_lora, -1)], axis=1)
            qall = _norm_matmul(cq, mla_q_latent_g[j], w_all, BF16)
            nope_blk = heads * HEAD_DIM // LANES
            rope_blk = heads * ROPE_HALF // LANES
            qr1, qr2 = _rope(qall, nope_blk, nope_blk + rope_blk, rope_blk, pos, freq)
            o = _mla_attention(qall.reshape(b, s, -1), qr1.reshape(b, s, -1), qr2.reshape(b, s, -1),
                               kv, kr1, kr2, heads)
            xf = _matmul_residual(o.reshape(n, -1), mla_w_o[j], xf)
        xf = _peer_ffn(xf, peer_norm_g[layer], peer_w_q[layer], peer_keys_1[layer],
                       peer_keys_2[layer], peer_u[layer], peer_v[layer])
    return _rmsnorm(xf, final_norm_g).reshape(b, s, d)
```

```python
import functools
import math

import jax
import jax.numpy as jnp
from jax import lax
from jax.experimental import pallas as pl
from jax.experimental.pallas import tpu as pltpu

F32 = jnp.float32
BF16 = jnp.bfloat16

EPS = 1e-6
F32_EXP_UNDERFLOW = -104.0
LANES = 128
HEAD_DIM = 128
ROPE_DIM = 64
ROPE_HALF = ROPE_DIM // 2
ROPE_BASE = 10000.0
ROPE_GROUP = LANES // ROPE_HALF
PEER_TOPK = 16
PEER_KEYS = 128
PEER_HALF = 128

ATTN_TILE = 256
HEAD_GROUP = 4
ROW_TILE = 1024
COL_TILE = 1024
ROUTE_TILE = 512
ROUTE_WIDTH = 256
GATHER_TOKENS = 8
VMEM_LIMIT = 56 * 1024 * 1024


def _cparams(*sem):
    return pltpu.CompilerParams(dimension_semantics=sem, vmem_limit_bytes=VMEM_LIMIT)


def _fit_tile(size, tile):
    tile = min(tile, size)
    while size % tile:
        tile //= 2
    return tile


def _dot(a, b):
    return jnp.dot(a, b, preferred_element_type=F32)


def _dot_nt(a, b):
    return lax.dot_general(a, b, (((1,), (1,)), ((), ())), preferred_element_type=F32)


def _norm_matmul_kernel(x_ref, g_ref, w_ref, o_ref, xn_ref):
    @pl.when(pl.program_id(1) == 0)
    def _():
        x = x_ref[...]
        ms = jnp.mean(x * x, axis=-1, keepdims=True)
        xn_ref[...] = (x * lax.rsqrt(ms + EPS) * g_ref[...]).astype(BF16)

    o_ref[...] = _dot(xn_ref[...], w_ref[...]).astype(o_ref.dtype)


def _norm_matmul(x, g, w, out_dtype, *, k=None, tn=COL_TILE):
    n = x.shape[0]
    k = x.shape[1] if k is None else k
    f = w.shape[1]
    tm = _fit_tile(n, ROW_TILE)
    tn = _fit_tile(f, tn)
    assert w.shape[0] == k
    return pl.pallas_call(
        _norm_matmul_kernel,
        grid=(n // tm, f // tn),
        in_specs=[
            pl.BlockSpec((tm, k), lambda i, j: (i, 0)),
            pl.BlockSpec((1, k), lambda i, j: (0, 0)),
            pl.BlockSpec((k, tn), lambda i, j: (0, j)),
        ],
        out_specs=pl.BlockSpec((tm, tn), lambda i, j: (i, j)),
        out_shape=jax.ShapeDtypeStruct((n, f), out_dtype),
        scratch_shapes=[pltpu.VMEM((tm, k), BF16)],
        compiler_params=_cparams("parallel", "arbitrary"),
        name="norm_matmul",
    )(x, g.reshape(1, k).astype(F32), w.astype(BF16))


def _matmul_residual_kernel(a_ref, w_ref, x_ref, o_ref):
    o_ref[...] = x_ref[...] + _dot(a_ref[...], w_ref[...])


def _matmul_residual(a, w, x, *, tn=COL_TILE):
    n, k = a.shape
    f = w.shape[1]
    tm = _fit_tile(n, ROW_TILE)
    tn = _fit_tile(f, tn)
    return pl.pallas_call(
        _matmul_residual_kernel,
        grid=(n // tm, f // tn),
        in_specs=[
            pl.BlockSpec((tm, k), lambda i, j: (i, 0)),
            pl.BlockSpec((k, tn), lambda i, j: (0, j)),
            pl.BlockSpec((tm, tn), lambda i, j: (i, j)),
        ],
        out_specs=pl.BlockSpec((tm, tn), lambda i, j: (i, j)),
        out_shape=jax.ShapeDtypeStruct((n, f), F32),
        compiler_params=_cparams("parallel", "arbitrary"),
        name="matmul_residual",
    )(a, w.astype(BF16), x)


def _rmsnorm_kernel(x_ref, g_ref, o_ref):
    x = x_ref[...]
    ms = jnp.mean(x * x, axis=-1, keepdims=True)
    o_ref[...] = x * lax.rsqrt(ms + EPS) * g_ref[...]


def _rmsnorm(x, g):
    n, d = x.shape
    tm = min(ROW_TILE, n)
    return pl.pallas_call(
        _rmsnorm_kernel,
        grid=(n // tm,),
        in_specs=[pl.BlockSpec((tm, d), lambda i: (i, 0)), pl.BlockSpec((1, d), lambda i: (0, 0))],
        out_specs=pl.BlockSpec((tm, d), lambda i: (i, 0)),
        out_shape=jax.ShapeDtypeStruct((n, d), F32),
        compiler_params=_cparams("parallel"),
        name="final_rmsnorm",
    )(x, g.reshape(1, d))


def _sb_attn_kernel(q_ref, k_ref, v_ref, o_ref, acc_ref, carry_ref, *, tile, scale, group):
    i = pl.program_id(2)
    row = lax.broadcasted_iota(jnp.int32, (tile, tile), 0)
    col = lax.broadcasted_iota(jnp.int32, (tile, tile), 1)
    later = (row > col).astype(BF16)
    causal = col < row

    def visit(j, masked):
        rows = pl.ds(pl.multiple_of(j * tile, tile), tile)
        lanes = [slice(g * HEAD_DIM, (g + 1) * HEAD_DIM) for g in range(group)]
        zs = [_dot_nt(q_ref[:, lanes[g]], k_ref[rows, lanes[g]]) * scale for g in range(group)]
        log_1m_betas, log_betas = [], []
        for z in zs:
            softplus = jnp.maximum(z, 0.0) + jnp.log(1.0 + jnp.exp(-jnp.abs(z)))
            log_1m_beta = -softplus
            if masked:
                log_1m_beta = jnp.where(causal, log_1m_beta, 0.0)
            log_1m_betas.append(log_1m_beta)
            log_betas.append(z - softplus)
        tails = []
        for log_1m_beta in log_1m_betas:
            hi = log_1m_beta.astype(BF16)
            lo = (log_1m_beta - hi.astype(F32)).astype(BF16)
            tails.append(_dot(hi, later) + _dot(lo, later))
        carries = [carry_ref[g] for g in range(group)]
        weights = []
        for g in range(group):
            a = jnp.exp(log_betas[g] + tails[g] + carries[g])
            if masked:
                a = jnp.where(causal, a, 0.0)
            weights.append(a.astype(BF16))
        outs = [_dot(weights[g], v_ref[rows, lanes[g]]) for g in range(group)]
        for g in range(group):
            acc_ref[:, lanes[g]] += outs[g]
            carry_ref[g] = carries[g] + tails[g][:, 0:1] + log_1m_betas[g][:, 0:1]

    acc_ref[...] = jnp.zeros_like(acc_ref)
    carry_ref[...] = jnp.zeros_like(carry_ref)
    visit(i, True)

    def any_row_alive():
        return jnp.max(carry_ref[...]) > F32_EXP_UNDERFLOW

    def cond(state):
        return jnp.logical_and(state[0] < i, state[1])

    def body(state):
        visit(i - 1 - state[0], False)
        return state[0] + 1, any_row_alive()

    lax.while_loop(cond, body, (0, any_row_alive()))
    o_ref[...] = acc_ref[...].astype(o_ref.dtype)


def _sb_attention(qkv, heads):
    b, s, _ = qkv.shape
    tile = _fit_tile(s, ATTN_TILE)
    group = _fit_tile(heads, HEAD_GROUP)
    groups = heads // group
    width = group * HEAD_DIM
    kern = functools.partial(_sb_attn_kernel, tile=tile, scale=HEAD_DIM ** -0.5, group=group)
    return pl.pallas_call(
        kern,
        grid=(b, groups, s // tile),
        in_specs=[
            pl.BlockSpec((None, tile, width), lambda bi, g, i: (bi, i, g)),
            pl.BlockSpec((None, s, width), lambda bi, g, i: (bi, 0, groups + g)),
            pl.BlockSpec((None, s, width), lambda bi, g, i: (bi, 0, 2 * groups + g)),
        ],
        out_specs=pl.BlockSpec((None, tile, width), lambda bi, g, i: (bi, i, g)),
        out_shape=jax.ShapeDtypeStruct((b, s, heads * HEAD_DIM), BF16),
        scratch_shapes=[pltpu.VMEM((tile, width), F32), pltpu.VMEM((group, tile, 1), F32)],
        compiler_params=_cparams("parallel", "parallel", "arbitrary"),
        name="sb_attention",
    )(qkv, qkv, qkv)


def _rope_kernel(x1_ref, x2_ref, pos_ref, freq_ref, o1_ref, o2_ref):
    ang = pos_ref[...] * freq_ref[...]
    c = jnp.cos(ang)
    s = jnp.sin(ang)
    x1 = x1_ref[...].astype(F32)
    x2 = x2_ref[...].astype(F32)
    o1_ref[...] = (x1 * c - x2 * s).astype(o1_ref.dtype)
    o2_ref[...] = (x2 * c + x1 * s).astype(o2_ref.dtype)


def _rope(src, col1, col2, nblk, pos, freq):
    n = src.shape[0]
    tm = min(ROW_TILE, n)
    out = jax.ShapeDtypeStruct((n, nblk * LANES), BF16)
    return pl.pallas_call(
        _rope_kernel,
        grid=(n // tm, nblk),
        in_specs=[
            pl.BlockSpec((tm, LANES), lambda i, j: (i, col1 + j)),
            pl.BlockSpec((tm, LANES), lambda i, j: (i, col2 + j)),
            pl.BlockSpec((tm, 1), lambda i, j: (i, 0)),
            pl.BlockSpec((1, LANES), lambda i, j: (0, 0)),
        ],
        out_specs=[pl.BlockSpec((tm, LANES), lambda i, j: (i, j))] * 2,
        out_shape=[out, out],
        compiler_params=_cparams("parallel", "parallel"),
        name="rope",
    )(src, src, pos, freq)


def _mla_attn_kernel(qn_ref, qr1_ref, qr2_ref, kv_ref, kr1_ref, kr2_ref, o_ref,
                     acc_ref, m_ref, l_ref, qr_ref, *, tile, scale):
    i = pl.program_id(2)
    lane = lax.broadcasted_iota(jnp.int32, (1, LANES), 1)
    row = lax.broadcasted_iota(jnp.int32, (tile, tile), 0)
    col = lax.broadcasted_iota(jnp.int32, (tile, tile), 1)
    causal = col <= row
    zero = jnp.zeros((), BF16)

    def visit(j, masked):
        rows = pl.ds(pl.multiple_of(j * tile, tile), tile)
        kr1 = kr1_ref[rows, :]
        kr2 = kr2_ref[rows, :]
        group = range(ROPE_GROUP)
        lanes = [slice(g * HEAD_DIM, (g + 1) * HEAD_DIM) for g in group]
        k_lanes = [slice(2 * g * HEAD_DIM, (2 * g + 1) * HEAD_DIM) for g in group]
        v_lanes = [slice((2 * g + 1) * HEAD_DIM, (2 * g + 2) * HEAD_DIM) for g in group]
        scores = []
        for g in group:
            sc = (_dot_nt(qn_ref[:, lanes[g]], kv_ref[rows, k_lanes[g]])
                  + _dot_nt(qr_ref[g, :, :LANES], kr1) + _dot_nt(qr_ref[g, :, LANES:], kr2)) * scale
            if masked:
                sc = jnp.where(causal, sc, -jnp.inf)
            scores.append(sc)
        m_old = [m_ref[g] for g in group]
        m_new = [jnp.maximum(m_old[g], jnp.max(scores[g], axis=-1, keepdims=True)) for g in group]
        probs = [jnp.exp(scores[g] - m_new[g]) for g in group]
        outs = [_dot(probs[g].astype(BF16), kv_ref[rows, v_lanes[g]]) for g in group]
        for g in group:
            alpha = jnp.exp(m_old[g] - m_new[g])
            l_ref[g] = alpha * l_ref[g] + jnp.sum(probs[g], axis=-1, keepdims=True)
            acc_ref[:, lanes[g]] = alpha * acc_ref[:, lanes[g]] + outs[g]
            m_ref[g] = m_new[g]

    for g in range(ROPE_GROUP):
        mine = (lane // ROPE_HALF) == g
        qr_ref[g] = jnp.concatenate([jnp.where(mine, qr1_ref[...], zero),
                                     jnp.where(mine, qr2_ref[...], zero)], axis=1)
    acc_ref[...] = jnp.zeros_like(acc_ref)
    l_ref[...] = jnp.zeros_like(l_ref)
    m_ref[...] = jnp.full_like(m_ref, -jnp.inf)
    visit(i, True)

    def body(n, c):
        visit(n, False)
        return c

    lax.fori_loop(0, i, body, 0)
    for g in range(ROPE_GROUP):
        lanes = slice(g * HEAD_DIM, (g + 1) * HEAD_DIM)
        o_ref[:, lanes] = (acc_ref[:, lanes] / l_ref[g]).astype(o_ref.dtype)


def _mla_attention(qn, qr1, qr2, kv, kr1, kr2, heads):
    b, s, _ = qn.shape
    tile = _fit_tile(s, ATTN_TILE)
    assert heads % ROPE_GROUP == 0
    width = ROPE_GROUP * HEAD_DIM
    kern = functools.partial(_mla_attn_kernel, tile=tile, scale=(HEAD_DIM + ROPE_DIM) ** -0.5)
    qspec = pl.BlockSpec((None, tile, width), lambda bi, g, i: (bi, i, g))
    rspec = pl.BlockSpec((None, tile, LANES), lambda bi, g, i: (bi, i, g))
    shared = pl.BlockSpec((None, s, LANES), lambda bi, g, i: (bi, 0, 0))
    return pl.pallas_call(
        kern,
        grid=(b, heads // ROPE_GROUP, s // tile),
        in_specs=[
            qspec, rspec, rspec,
            pl.BlockSpec((None, s, 2 * width), lambda bi, g, i: (bi, 0, g)),
            shared, shared,
        ],
        out_specs=qspec,
        out_shape=jax.ShapeDtypeStruct((b, s, heads * HEAD_DIM), BF16),
        scratch_shapes=[pltpu.VMEM((tile, width), F32),
                        pltpu.VMEM((ROPE_GROUP, tile, 1), F32),
                        pltpu.VMEM((ROPE_GROUP, tile, 1), F32),
                        pltpu.VMEM((ROPE_GROUP, tile, 2 * LANES), BF16)],
        compiler_params=_cparams("parallel", "parallel", "arbitrary"),
        name="mla_attention",
    )(qn, qr1, qr2, kv, kr1, kr2)


def _topk_rows(s, k, payload=None):
    rows = s.shape[0]
    iota = lax.broadcasted_iota(jnp.int32, s.shape, 0).astype(F32)
    vals, picked = [], []
    for _ in range(k):
        m = jnp.max(s, axis=0, keepdims=True)
        first = jnp.min(jnp.where(s == m, iota, float(rows)), axis=0, keepdims=True)
        hit = iota == first
        if payload is None:
            picked.append(first)
        else:
            picked.append(jnp.max(jnp.where(hit, payload, -1.0), axis=0, keepdims=True))
        vals.append(m)
        s = jnp.where(hit, -jnp.inf, s)
    return jnp.concatenate(vals, axis=0), jnp.concatenate(picked, axis=0)


def _pruned_candidates(v1, i1, v2, i2):
    assert PEER_TOPK == 16
    sub = lax.broadcasted_iota(jnp.int32, (8, v2.shape[1]), 0)
    scores = [v1[0:1] + v2]
    ids = [i1[0:1] * float(PEER_KEYS) + i2]
    for a in range(1, 8):
        keep = PEER_TOPK // (a + 1)
        blk = v1[a:a + 1] + v2[0:8]
        if keep < 8:
            blk = jnp.where(sub < keep, blk, -jnp.inf)
        scores.append(blk)
        ids.append(i1[a:a + 1] * float(PEER_KEYS) + i2[0:8])
    scores.append(v1[8:16] + v2[0:1])
    ids.append(i1[8:16] * float(PEER_KEYS) + i2[0:1])
    return jnp.concatenate(scores, axis=0), jnp.concatenate(ids, axis=0)


def _peer_route_kernel(q_ref, k1_ref, k2_ref, ids_ref, gate_ref, *, chunks, width, n_experts):
    def chunk(c, carry):
        qc = q_ref[pl.ds(pl.multiple_of(c * width, width), width), :]
        s1 = _dot_nt(k1_ref[...], qc[:, :PEER_HALF])
        s2 = _dot_nt(k2_ref[...], qc[:, PEER_HALF:])
        v1, i1 = _topk_rows(s1, PEER_TOPK)
        v2, i2 = _topk_rows(s2, PEER_TOPK)
        cand, cand_id = _pruned_candidates(v1, i1, v2, i2)
        top, ids = _topk_rows(cand, PEER_TOPK, cand_id)
        e = jnp.exp(top - top[0:1])
        gates = e / jnp.sum(e, axis=0, keepdims=True)
        ids = jnp.clip(ids.astype(jnp.int32), 0, n_experts - 1)
        for part in range(width // LANES):
            lanes = slice(part * LANES, (part + 1) * LANES)
            gate_ref[c * (width // LANES) + part] = gates[:, lanes]
            ids_ref[c * (width // LANES) + part] = ids[:, lanes]
        return carry

    lax.fori_loop(0, chunks, chunk, 0)


def _peer_route(q, keys_1, keys_2, n_experts):
    n = q.shape[0]
    heads = keys_1.shape[0]
    tm = _fit_tile(n, ROUTE_TILE)
    width = _fit_tile(tm, ROUTE_WIDTH)
    kern = functools.partial(_peer_route_kernel, chunks=tm // width, width=width, n_experts=n_experts)
    ospec = pl.BlockSpec((tm // LANES, PEER_TOPK, LANES), lambda i, h: (i, h, 0))
    kspec = pl.BlockSpec((None, PEER_KEYS, PEER_HALF), lambda i, h: (h, 0, 0))
    return pl.pallas_call(
        kern,
        grid=(n // tm, heads),
        in_specs=[pl.BlockSpec((tm, 2 * PEER_HALF), lambda i, h: (i, h)), kspec, kspec],
        out_specs=[ospec, ospec],
        out_shape=[
            jax.ShapeDtypeStruct((n // LANES, heads * PEER_TOPK, LANES), jnp.int32),
            jax.ShapeDtypeStruct((n // LANES, heads * PEER_TOPK, LANES), F32),
        ],
        compiler_params=_cparams("parallel", "parallel"),
        name="peer_route",
    )(q, keys_1.astype(BF16), keys_2.astype(BF16))


def _peer_expert_kernel(ids_cur_ref, ids_nxt_ref, x_ref, g_ref, gate_ref, uv_hbm, o_ref,
                        uvbuf, wbuf, sem, *, tokens, n_sel, slabs, d_model):
    i = pl.program_id(0)
    n_steps = pl.num_programs(0)
    slot = i % 2
    nxt = 1 - slot

    def row_copy(ids_ref, to_slot, t, e):
        return pltpu.make_async_copy(
            uv_hbm.at[ids_ref[0, t, e]], uvbuf.at[to_slot, t * n_sel + e], sem.at[to_slot, t])

    def token_rows_wait(in_slot, t):
        rows = uvbuf.at[in_slot, pl.ds(pl.multiple_of(t * n_sel, n_sel), n_sel)]
        pltpu.make_async_copy(uv_hbm.at[pl.ds(0, n_sel)], rows, sem.at[in_slot, t]).wait()
        return rows

    @pl.when(i == 0)
    def _():
        def body(e, carry):
            for t in range(tokens):
                row_copy(ids_cur_ref, 0, t, e).start()
            return carry
        lax.fori_loop(0, n_sel, body, 0)

    eye = (lax.broadcasted_iota(jnp.int32, (n_sel, n_sel), 0)
           == lax.broadcasted_iota(jnp.int32, (n_sel, n_sel), 1))
    norm_g = g_ref[...]

    def token(t, carry):
        rows = token_rows_wait(slot, t)
        xt = x_ref[t]
        ms = jnp.sum(jnp.sum(xt * xt, axis=0, keepdims=True), axis=1, keepdims=True) / d_model
        ht = xt * lax.rsqrt(ms + EPS) * norm_g
        parts = []
        for e0 in range(0, n_sel, 8):
            for e in range(e0, e0 + 8):
                row_copy(ids_nxt_ref, nxt, t, e).start()
            u = rows[e0:e0 + 8, 0:slabs, :].astype(F32)
            parts.append(jnp.sum(u * ht[None], axis=1))
        part = jnp.concatenate(parts, axis=0)
        s = jnp.sum(part, axis=-1, keepdims=True)
        act = 0.5 * s * (1.0 + lax.erf(s * (2.0 ** -0.5)))
        gate_row = gate_ref[pl.ds(t, 1), :]
        gate_col = jnp.sum(jnp.where(eye, gate_row, 0.0), axis=-1, keepdims=True)
        wbuf[...] = jnp.broadcast_to(act * gate_col, (n_sel, LANES))
        sums = [xt] + [jnp.zeros_like(xt)] * 3
        for e in range(n_sel):
            sums[e % 4] = sums[e % 4] + wbuf[e:e + 1, :] * rows[e, slabs:2 * slabs, :].astype(F32)
        o_ref[t] = (sums[0] + sums[1]) + (sums[2] + sums[3])
        return carry

    lax.fori_loop(0, tokens, token, 0)

    @pl.when(i == n_steps - 1)
    def _():
        for t in range(tokens):
            token_rows_wait(nxt, t)


def _peer_experts(x, norm_g, ids, gates, uv):
    n, d = x.shape
    n_sel = ids.shape[1]
    slabs = d // LANES
    tt = min(GATHER_TOKENS, n)
    steps = n // tt
    kern = functools.partial(_peer_expert_kernel, tokens=tt, n_sel=n_sel, slabs=slabs, d_model=d)
    ids3 = ids.reshape(steps, tt, n_sel)
    out = pl.pallas_call(
        kern,
        grid=(steps,),
        in_specs=[
            pl.BlockSpec((1, tt, n_sel), lambda i: (i, 0, 0), memory_space=pltpu.SMEM),
            pl.BlockSpec((1, tt, n_sel), lambda i: (jnp.minimum(i + 1, steps - 1), 0, 0),
                         memory_space=pltpu.SMEM),
            pl.BlockSpec((tt, slabs, LANES), lambda i: (i, 0, 0)),
            pl.BlockSpec((slabs, LANES), lambda i: (0, 0)),
            pl.BlockSpec((tt, n_sel), lambda i: (i, 0)),
            pl.BlockSpec(memory_space=pl.ANY),
        ],
        out_specs=pl.BlockSpec((tt, slabs, LANES), lambda i: (i, 0, 0)),
        out_shape=jax.ShapeDtypeStruct((n, slabs, LANES), F32),
        scratch_shapes=[
            pltpu.VMEM((2, tt * n_sel, 2 * slabs, LANES), uv.dtype),
            pltpu.VMEM((n_sel, LANES), F32),
            pltpu.SemaphoreType.DMA((2, tt)),
        ],
        compiler_params=_cparams("arbitrary"),
        name="peer_experts",
    )(ids3, ids3, x.reshape(n, slabs, LANES), norm_g.reshape(slabs, LANES), gates, uv)
    return out.reshape(n, d)


def _peer_ffn(x, norm_g, w_q, keys_1, keys_2, table_u, table_v):
    n, d = x.shape
    n_experts = table_u.shape[0]
    q = _norm_matmul(x, norm_g, w_q, BF16)
    ids, gates = _peer_route(q, keys_1, keys_2, n_experts)
    ids = jnp.swapaxes(ids, 1, 2).reshape(n, -1)
    gates = jnp.swapaxes(gates, 1, 2).reshape(n, -1)
    slabs = d // LANES
    uv = jnp.concatenate([table_u.reshape(n_experts, slabs, LANES),
                          table_v.reshape(n_experts, slabs, LANES)], axis=1).astype(BF16)
    return _peer_experts(x, norm_g, ids, gates, uv)


def _tile_cols(w, reps):
    return jnp.concatenate([w] * reps, axis=1)


def kernel(x, positions, sb_norm_g, sb_w_qkv, sb_w_o, kv_norm_g, mla_w_dkv, mla_kv_latent_g,
           mla_w_ukv, mla_norm_g, mla_w_dq, mla_q_latent_g, mla_w_uq, mla_w_o, peer_norm_g,
           peer_w_q, peer_keys_1, peer_keys_2, peer_u, peer_v, final_norm_g):
    b, s, d = x.shape
    n = b * s
    depth = peer_norm_g.shape[0]
    n_a = sb_norm_g.shape[0]
    xf = x.reshape(n, d)
    pos = positions.reshape(n, 1).astype(F32)
    inv_freq = ROPE_BASE ** (-jnp.arange(0, ROPE_DIM, 2, dtype=F32) / ROPE_DIM)
    freq = jnp.tile(inv_freq, ROPE_GROUP).reshape(1, LANES)

    kv = kr1 = kr2 = None
    for layer in range(depth):
        if layer < n_a:
            heads = sb_w_qkv.shape[2] // (3 * HEAD_DIM)
            qkv = _norm_matmul(xf, sb_norm_g[layer], sb_w_qkv[layer], BF16)
            o = _sb_attention(qkv.reshape(b, s, -1), heads)
            xf = _matmul_residual(o.reshape(n, -1), sb_w_o[layer], xf)
        else:
            kv_lora = mla_kv_latent_g.shape[0]
            heads = mla_w_ukv.shape[1] // (2 * HEAD_DIM)
            if layer == n_a:
                w_dkv = jnp.concatenate([
                    mla_w_dkv[:, :kv_lora],
                    _tile_cols(mla_w_dkv[:, kv_lora:kv_lora + ROPE_HALF], ROPE_GROUP),
                    _tile_cols(mla_w_dkv[:, kv_lora + ROPE_HALF:], ROPE_GROUP)], axis=1)
                ckv = _norm_matmul(xf, kv_norm_g, w_dkv, F32)
                kv = _norm_matmul(ckv, mla_kv_latent_g, mla_w_ukv, BF16, k=kv_lora).reshape(b, s, -1)
                blk = kv_lora // LANES
                kr1, kr2 = _rope(ckv, blk, blk + 1, 1, pos, freq)
                kr1 = kr1.reshape(b, s, LANES)
                kr2 = kr2.reshape(b, s, LANES)
            j = layer - n_a
            q_lora = mla_q_latent_g.shape[1]
            cq = _norm_matmul(xf, mla_norm_g[j], mla_w_dq[j], F32)
            w_uq = mla_w_uq[j].reshape(q_lora, heads, HEAD_DIM + ROPE_DIM)
            w_all = jnp.concatenate([
                w_uq[:, :, :HEAD_DIM].reshape(q_lora, -1),
                w_uq[:, :, HEAD_DIM:HEAD_DIM + ROPE_HALF].reshape(q_lora, -1),
                w_uq[:, :, HEAD_DIM + ROPE_HALF:].reshape(q_lora, -1)], axis=1)
            qall = _norm_matmul(cq, mla_q_latent_g[j], w_all, BF16)
            nope_blk = heads * HEAD_DIM // LANES
            rope_blk = heads * ROPE_HALF // LANES
            qr1, qr2 = _rope(qall, nope_blk, nope_blk + rope_blk, rope_blk, pos, freq)
            o = _mla_attention(qall.reshape(b, s, -1), qr1.reshape(b, s, -1), qr2.reshape(b, s, -1),
                               kv, kr1, kr2, heads)
            xf = _matmul_residual(o.reshape(n, -1), mla_w_o[j], xf)
        xf = _peer_ffn(xf, peer_norm_g[layer], peer_w_q[layer], peer_keys_1[layer],
                       peer_keys_2[layer], peer_u[layer], peer_v[layer])
    return _rmsnorm(xf, final_norm_g).reshape(b, s, d)
```

```python
import functools
import math

import jax
import jax.numpy as jnp
from jax import lax
from jax.experimental import pallas as pl
from jax.experimental.pallas import tpu as pltpu

F32 = jnp.float32
BF16 = jnp.bfloat16

EPS = 1e-6
F32_EXP_UNDERFLOW = -104.0
LANES = 128
HEAD_DIM = 128
ROPE_DIM = 64
ROPE_HALF = ROPE_DIM // 2
ROPE_BASE = 10000.0
ROPE_GROUP = LANES // ROPE_HALF
PEER_TOPK = 16
PEER_KEYS = 128
PEER_HALF = 128

ATTN_TILE = 256
HEAD_GROUP = 4
ROW_TILE = 1024
COL_TILE = 1024
ROUTE_TILE = 512
ROUTE_WIDTH = 256
GATHER_TOKENS = 8
VMEM_LIMIT = 56 * 1024 * 1024


def _cparams(*sem):
    return pltpu.CompilerParams(dimension_semantics=sem, vmem_limit_bytes=VMEM_LIMIT)


def _fit_tile(size, tile):
    tile = min(tile, size)
    while size % tile:
        tile //= 2
    return tile


def _dot(a, b):
    return jnp.dot(a, b, preferred_element_type=F32)


def _dot_nt(a, b):
    return lax.dot_general(a, b, (((1,), (1,)), ((), ())), preferred_element_type=F32)


def _norm_matmul_kernel(x_ref, g_ref, w_ref, o_ref, xn_ref):
    @pl.when(pl.program_id(1) == 0)
    def _():
        x = x_ref[...]
        ms = jnp.mean(x * x, axis=-1, keepdims=True)
        xn_ref[...] = (x * lax.rsqrt(ms + EPS) * g_ref[...]).astype(BF16)

    o_ref[...] = _dot(xn_ref[...], w_ref[...]).astype(o_ref.dtype)


def _norm_matmul(x, g, w, out_dtype, *, k=None, tn=COL_TILE):
    n = x.shape[0]
    k = x.shape[1] if k is None else k
    f = w.shape[1]
    tm = _fit_tile(n, ROW_TILE)
    tn = _fit_tile(f, tn)
    assert w.shape[0] == k
    return pl.pallas_call(
        _norm_matmul_kernel,
        grid=(n // tm, f // tn),
        in_specs=[
            pl.BlockSpec((tm, k), lambda i, j: (i, 0)),
            pl.BlockSpec((1, k), lambda i, j: (0, 0)),
            pl.BlockSpec((k, tn), lambda i, j: (0, j)),
        ],
        out_specs=pl.BlockSpec((tm, tn), lambda i, j: (i, j)),
        out_shape=jax.ShapeDtypeStruct((n, f), out_dtype),
        scratch_shapes=[pltpu.VMEM((tm, k), BF16)],
        compiler_params=_cparams("parallel", "arbitrary"),
        name="norm_matmul",
    )(x, g.reshape(1, k).astype(F32), w.astype(BF16))


def _matmul_residual_kernel(a_ref, w_ref, x_ref, o_ref):
    o_ref[...] = x_ref[...] + _dot(a_ref[...], w_ref[...])


def _matmul_residual(a, w, x, *, tn=COL_TILE):
    n, k = a.shape
    f = w.shape[1]
    tm = _fit_tile(n, ROW_TILE)
    tn = _fit_tile(f, tn)
    return pl.pallas_call(
        _matmul_residual_kernel,
        grid=(n // tm, f // tn),
        in_specs=[
            pl.BlockSpec((tm, k), lambda i, j: (i, 0)),
            pl.BlockSpec((k, tn), lambda i, j: (0, j)),
            pl.BlockSpec((tm, tn), lambda i, j: (i, j)),
        ],
        out_specs=pl.BlockSpec((tm, tn), lambda i, j: (i, j)),
        out_shape=jax.ShapeDtypeStruct((n, f), F32),
        compiler_params=_cparams("parallel", "arbitrary"),
        name="matmul_residual",
    )(a, w.astype(BF16), x)


def _rmsnorm_kernel(x_ref, g_ref, o_ref):
    x = x_ref[...]
    ms = jnp.mean(x * x, axis=-1, keepdims=True)
    o_ref[...] = x * lax.rsqrt(ms + EPS) * g_ref[...]


def _rmsnorm(x, g):
    n, d = x.shape
    tm = min(ROW_TILE, n)
    return pl.pallas_call(
        _rmsnorm_kernel,
        grid=(n // tm,),
        in_specs=[pl.BlockSpec((tm, d), lambda i: (i, 0)), pl.BlockSpec((1, d), lambda i: (0, 0))],
        out_specs=pl.BlockSpec((tm, d), lambda i: (i, 0)),
        out_shape=jax.ShapeDtypeStruct((n, d), F32),
        compiler_params=_cparams("parallel"),
        name="final_rmsnorm",
    )(x, g.reshape(1, d))


def _sb_attn_kernel(q_ref, k_ref, v_ref, o_ref, acc_ref, carry_ref, *, tile, scale, group):
    i = pl.program_id(2)
    row = lax.broadcasted_iota(jnp.int32, (tile, tile), 0)
    col = lax.broadcasted_iota(jnp.int32, (tile, tile), 1)
    later = (row > col).astype(BF16)
    causal = col < row

    def visit(j, masked):
        rows = pl.ds(pl.multiple_of(j * tile, tile), tile)
        lanes = [slice(g * HEAD_DIM, (g + 1) * HEAD_DIM) for g in range(group)]
        zs = [_dot_nt(q_ref[:, lanes[g]], k_ref[rows, lanes[g]]) * scale for g in range(group)]
        log_1m_betas, log_betas = [], []
        for z in zs:
            softplus = jnp.maximum(z, 0.0) + jnp.log(1.0 + jnp.exp(-jnp.abs(z)))
            log_1m_beta = -softplus
            if masked:
                log_1m_beta = jnp.where(causal, log_1m_beta, 0.0)
            log_1m_betas.append(log_1m_beta)
            log_betas.append(z - softplus)
        tails = []
        for log_1m_beta in log_1m_betas:
            hi = log_1m_beta.astype(BF16)
            lo = (log_1m_beta - hi.astype(F32)).astype(BF16)
            tails.append(_dot(hi, later) + _dot(lo, later))
        carries = [carry_ref[g] for g in range(group)]
        weights = []
        for g in range(group):
            a = jnp.exp(log_betas[g] + tails[g] + carries[g])
            if masked:
                a = jnp.where(causal, a, 0.0)
            weights.append(a.astype(BF16))
        outs = [_dot(weights[g], v_ref[rows, lanes[g]]) for g in range(group)]
        for g in range(group):
            acc_ref[:, lanes[g]] += outs[g]
            carry_ref[g] = carries[g] + tails[g][:, 0:1] + log_1m_betas[g][:, 0:1]

    acc_ref[...] = jnp.zeros_like(acc_ref)
    carry_ref[...] = jnp.zeros_like(carry_ref)
    visit(i, True)

    def any_row_alive():
        return jnp.max(carry_ref[...]) > F32_EXP_UNDERFLOW

    def cond(state):
        return jnp.logical_and(state[0] < i, state[1])

    def body(state):
        visit(i - 1 - state[0], False)
        return state[0] + 1, any_row_alive()

    lax.while_loop(cond, body, (0, any_row_alive()))
    o_ref[...] = acc_ref[...].astype(o_ref.dtype)


def _sb_attention(qkv, heads):
    b, s, _ = qkv.shape
    tile = _fit_tile(s, ATTN_TILE)
    group = _fit_tile(heads, HEAD_GROUP)
    groups = heads // group
    width = group * HEAD_DIM
    kern = functools.partial(_sb_attn_kernel, tile=tile, scale=HEAD_DIM ** -0.5, group=group)
    return pl.pallas_call(
        kern,
        grid=(b, groups, s // tile),
        in_specs=[
            pl.BlockSpec((None, tile, width), lambda bi, g, i: (bi, i, g)),
            pl.BlockSpec((None, s, width), lambda bi, g, i: (bi, 0, groups + g)),
            pl.BlockSpec((None, s, width), lambda bi, g, i: (bi, 0, 2 * groups + g)),
        ],
        out_specs=pl.BlockSpec((None, tile, width), lambda bi, g, i: (bi, i, g)),
        out_shape=jax.ShapeDtypeStruct((b, s, heads * HEAD_DIM), BF16),
        scratch_shapes=[pltpu.VMEM((tile, width), F32), pltpu.VMEM((group, tile, 1), F32)],
        compiler_params=_cparams("parallel", "parallel", "arbitrary"),
        name="sb_attention",
    )(qkv, qkv, qkv)


def _rope_kernel(x1_ref, x2_ref, pos_ref, freq_ref, o1_ref, o2_ref):
    ang = pos_ref[...] * freq_ref[...]
    c = jnp.cos(ang)
    s = jnp.sin(ang)
    x1 = x1_ref[...].astype(F32)
    x2 = x2_ref[...].astype(F32)
    o1_ref[...] = (x1 * c - x2 * s).astype(o1_ref.dtype)
    o2_ref[...] = (x2 * c + x1 * s).astype(o2_ref.dtype)


def _rope(src, col1, col2, nblk, pos, freq):
    n = src.shape[0]
    tm = min(ROW_TILE, n)
    out = jax.ShapeDtypeStruct((n, nblk * LANES), BF16)
    return pl.pallas_call(
        _rope_kernel,
        grid=(n // tm, nblk),
        in_specs=[
            pl.BlockSpec((tm, LANES), lambda i, j: (i, col1 + j)),
            pl.BlockSpec((tm, LANES), lambda i, j: (i, col2 + j)),
            pl.BlockSpec((tm, 1), lambda i, j: (i, 0)),
            pl.BlockSpec((1, LANES), lambda i, j: (0, 0)),
        ],
        out_specs=[pl.BlockSpec((tm, LANES), lambda i, j: (i, j))] * 2,
        out_shape=[out, out],
        compiler_params=_cparams("parallel", "parallel"),
        name="rope",
    )(src, src, pos, freq)


def _mla_attn_kernel(qn_ref, qr1_ref, qr2_ref, kv_ref, kr1_ref, kr2_ref, o_ref,
                     acc_ref, m_ref, l_ref, qr_ref, *, tile, ktile, scale):
    i = pl.program_id(2)
    lane = lax.broadcasted_iota(jnp.int32, (1, LANES), 1)
    row = lax.broadcasted_iota(jnp.int32, (tile, ktile), 0)
    col = lax.broadcasted_iota(jnp.int32, (tile, ktile), 1)
    diag = (i * tile) // ktile
    causal = diag * ktile + col <= i * tile + row
    zero = jnp.zeros((), BF16)

    def visit(j, masked):
        rows = pl.ds(pl.multiple_of(j * ktile, ktile), ktile)
        kr1 = kr1_ref[rows, :]
        kr2 = kr2_ref[rows, :]
        group = range(ROPE_GROUP)
        lanes = [slice(g * HEAD_DIM, (g + 1) * HEAD_DIM) for g in group]
        k_lanes = [slice(2 * g * HEAD_DIM, (2 * g + 1) * HEAD_DIM) for g in group]
        v_lanes = [slice((2 * g + 1) * HEAD_DIM, (2 * g + 2) * HEAD_DIM) for g in group]
        scores = []
        for g in group:
            sc = (_dot_nt(qn_ref[:, lanes[g]], kv_ref[rows, k_lanes[g]])
                  + _dot_nt(qr_ref[g, :, :LANES], kr1) + _dot_nt(qr_ref[g, :, LANES:], kr2)) * scale
            if masked:
                sc = jnp.where(causal, sc, -jnp.inf)
            scores.append(sc)
        m_old = [m_ref[g] for g in group]
        m_new = [jnp.maximum(m_old[g], jnp.max(scores[g], axis=-1, keepdims=True)) for g in group]
        probs = [jnp.exp(scores[g] - m_new[g]) for g in group]
        outs = [_dot(probs[g].astype(BF16), kv_ref[rows, v_lanes[g]]) for g in group]
        for g in group:
            alpha = jnp.exp(m_old[g] - m_new[g])
            l_ref[g] = alpha * l_ref[g] + jnp.sum(probs[g], axis=-1, keepdims=True)
            acc_ref[:, lanes[g]] = alpha * acc_ref[:, lanes[g]] + outs[g]
            m_ref[g] = m_new[g]

    for g in range(ROPE_GROUP):
        mine = (lane // ROPE_HALF) == g
        qr_ref[g] = jnp.concatenate([jnp.where(mine, qr1_ref[...], zero),
                                     jnp.where(mine, qr2_ref[...], zero)], axis=1)
    acc_ref[...] = jnp.zeros_like(acc_ref)
    l_ref[...] = jnp.zeros_like(l_ref)
    m_ref[...] = jnp.full_like(m_ref, -jnp.inf)
    visit(diag, True)

    def body(n, c):
        visit(n, False)
        return c

    lax.fori_loop(0, diag, body, 0)
    for g in range(ROPE_GROUP):
        lanes = slice(g * HEAD_DIM, (g + 1) * HEAD_DIM)
        o_ref[:, lanes] = (acc_ref[:, lanes] / l_ref[g]).astype(o_ref.dtype)


def _mla_attention(qn, qr1, qr2, kv, kr1, kr2, heads):
    b, s, _ = qn.shape
    tile = _fit_tile(s, ATTN_TILE)
    assert heads % ROPE_GROUP == 0
    width = ROPE_GROUP * HEAD_DIM
    ktile = 2 * tile if s % (2 * tile) == 0 else tile
    kern = functools.partial(_mla_attn_kernel, tile=tile, ktile=ktile,
                             scale=(HEAD_DIM + ROPE_DIM) ** -0.5)
    qspec = pl.BlockSpec((None, tile, width), lambda bi, g, i: (bi, i, g))
    rspec = pl.BlockSpec((None, tile, LANES), lambda bi, g, i: (bi, i, g))
    shared = pl.BlockSpec((None, s, LANES), lambda bi, g, i: (bi, 0, 0))
    return pl.pallas_call(
        kern,
        grid=(b, heads // ROPE_GROUP, s // tile),
        in_specs=[
            qspec, rspec, rspec,
            pl.BlockSpec((None, s, 2 * width), lambda bi, g, i: (bi, 0, g)),
            shared, shared,
        ],
        out_specs=qspec,
        out_shape=jax.ShapeDtypeStruct((b, s, heads * HEAD_DIM), BF16),
        scratch_shapes=[pltpu.VMEM((tile, width), F32),
                        pltpu.VMEM((ROPE_GROUP, tile, 1), F32),
                        pltpu.VMEM((ROPE_GROUP, tile, 1), F32),
                        pltpu.VMEM((ROPE_GROUP, tile, 2 * LANES), BF16)],
        compiler_params=_cparams("parallel", "parallel", "arbitrary"),
        name="mla_attention",
    )(qn, qr1, qr2, kv, kr1, kr2)


def _topk_rows(s, k, payload=None):
    rows = s.shape[0]
    iota = lax.broadcasted_iota(jnp.int32, s.shape, 0).astype(F32)
    vals, picked = [], []
    for _ in range(k):
        m = jnp.max(s, axis=0, keepdims=True)
        first = jnp.min(jnp.where(s == m, iota, float(rows)), axis=0, keepdims=True)
        hit = iota == first
        if payload is None:
            picked.append(first)
        else:
            picked.append(jnp.max(jnp.where(hit, payload, -1.0), axis=0, keepdims=True))
        vals.append(m)
        s = jnp.where(hit, -jnp.inf, s)
    return jnp.concatenate(vals, axis=0), jnp.concatenate(picked, axis=0)


def _pruned_candidates(v1, i1, v2, i2):
    assert PEER_TOPK == 16
    sub = lax.broadcasted_iota(jnp.int32, (8, v2.shape[1]), 0)
    scores = [v1[0:1] + v2]
    ids = [i1[0:1] * float(PEER_KEYS) + i2]
    for a in range(1, 8):
        keep = PEER_TOPK // (a + 1)
        blk = v1[a:a + 1] + v2[0:8]
        if keep < 8:
            blk = jnp.where(sub < keep, blk, -jnp.inf)
        scores.append(blk)
        ids.append(i1[a:a + 1] * float(PEER_KEYS) + i2[0:8])
    scores.append(v1[8:16] + v2[0:1])
    ids.append(i1[8:16] * float(PEER_KEYS) + i2[0:1])
    return jnp.concatenate(scores, axis=0), jnp.concatenate(ids, axis=0)


def _peer_route_kernel(q_ref, k1_ref, k2_ref, ids_ref, gate_ref, *, chunks, width, n_experts):
    def chunk(c, carry):
        qc = q_ref[pl.ds(pl.multiple_of(c * width, width), width), :]
        s1 = _dot_nt(k1_ref[...], qc[:, :PEER_HALF])
        s2 = _dot_nt(k2_ref[...], qc[:, PEER_HALF:])
        v1, i1 = _topk_rows(s1, PEER_TOPK)
        v2, i2 = _topk_rows(s2, PEER_TOPK)
        cand, cand_id = _pruned_candidates(v1, i1, v2, i2)
        top, ids = _topk_rows(cand, PEER_TOPK, cand_id)
        e = jnp.exp(top - top[0:1])
        gates = e / jnp.sum(e, axis=0, keepdims=True)
        ids = jnp.clip(ids.astype(jnp.int32), 0, n_experts - 1)
        for part in range(width // LANES):
            lanes = slice(part * LANES, (part + 1) * LANES)
            gate_ref[c * (width // LANES) + part] = gates[:, lanes]
            ids_ref[c * (width // LANES) + part] = ids[:, lanes]
        return carry

    lax.fori_loop(0, chunks, chunk, 0)


def _peer_route(q, keys_1, keys_2, n_experts):
    n = q.shape[0]
    heads = keys_1.shape[0]
    tm = _fit_tile(n, ROUTE_TILE)
    width = _fit_tile(tm, ROUTE_WIDTH)
    kern = functools.partial(_peer_route_kernel, chunks=tm // width, width=width, n_experts=n_experts)
    ospec = pl.BlockSpec((tm // LANES, PEER_TOPK, LANES), lambda i, h: (i, h, 0))
    kspec = pl.BlockSpec((None, PEER_KEYS, PEER_HALF), lambda i, h: (h, 0, 0))
    return pl.pallas_call(
        kern,
        grid=(n // tm, heads),
        in_specs=[pl.BlockSpec((tm, 2 * PEER_HALF), lambda i, h: (i, h)), kspec, kspec],
        out_specs=[ospec, ospec],
        out_shape=[
            jax.ShapeDtypeStruct((n // LANES, heads * PEER_TOPK, LANES), jnp.int32),
            jax.ShapeDtypeStruct((n // LANES, heads * PEER_TOPK, LANES), F32),
        ],
        compiler_params=_cparams("parallel", "parallel"),
        name="peer_route",
    )(q, keys_1.astype(BF16), keys_2.astype(BF16))


def _sublane_sums(tiles, sub):
    order = [0, 4, 2, 6, 1, 5, 3, 7]
    p = [tiles[k] for k in order]
    lo4 = sub < 4
    halves = []
    for a, b in ((p[0], p[1]), (p[2], p[3]), (p[4], p[5]), (p[6], p[7])):
        halves.append(jnp.where(lo4, a, b) + pltpu.roll(jnp.where(lo4, b, a), 4, axis=0))
    lo2 = (sub % 4) < 2
    quarters = []
    for m1, m2 in ((halves[0], halves[1]), (halves[2], halves[3])):
        t1 = m1 + pltpu.roll(m1, 6, axis=0)
        t2 = m2 + pltpu.roll(m2, 6, axis=0)
        quarters.append(jnp.where(lo2, t1, pltpu.roll(t2, 2, axis=0)))
    even = (sub % 2) == 0
    u1 = quarters[0] + pltpu.roll(quarters[0], 7, axis=0)
    u2 = quarters[1] + pltpu.roll(quarters[1], 7, axis=0)
    return jnp.where(even, u1, pltpu.roll(u2, 1, axis=0))


def _peer_expert_kernel(ids_cur_ref, ids_nxt_ref, x_ref, g_ref, gate_ref, uv_hbm, o_ref,
                        uvbuf, wbuf, sem, *, tokens, n_sel, slabs, d_model):
    i = pl.program_id(0)
    n_steps = pl.num_programs(0)
    slot = i % 2
    nxt = 1 - slot

    def row_copy(ids_ref, to_slot, t, e):
        return pltpu.make_async_copy(
            uv_hbm.at[ids_ref[0, t, e]], uvbuf.at[to_slot, t * n_sel + e], sem.at[to_slot, t])

    def token_rows_wait(in_slot, t):
        rows = uvbuf.at[in_slot, pl.ds(pl.multiple_of(t * n_sel, n_sel), n_sel)]
        pltpu.make_async_copy(uv_hbm.at[pl.ds(0, n_sel)], rows, sem.at[in_slot, t]).wait()
        return rows

    @pl.when(i == 0)
    def _():
        def body(e, carry):
            for t in range(tokens):
                row_copy(ids_cur_ref, 0, t, e).start()
            return carry
        lax.fori_loop(0, n_sel, body, 0)

    eye = (lax.broadcasted_iota(jnp.int32, (n_sel, n_sel), 0)
           == lax.broadcasted_iota(jnp.int32, (n_sel, n_sel), 1))
    norm_g = g_ref[...]
    sub = lax.broadcasted_iota(jnp.int32, (8, LANES), 0)

    def token(t, carry):
        rows = token_rows_wait(slot, t)
        xt = x_ref[t]
        ms = jnp.sum(jnp.sum(xt * xt, axis=0, keepdims=True), axis=1, keepdims=True) / d_model
        ht = xt * lax.rsqrt(ms + EPS) * norm_g
        parts = []
        for e0 in range(0, n_sel, 8):
            for e in range(e0, e0 + 8):
                row_copy(ids_nxt_ref, nxt, t, e).start()
            if slabs % 8 == 0:
                prods = []
                for e in range(e0, e0 + 8):
                    p = rows[e, 0:slabs, :].astype(F32) * ht
                    prods.append(sum(p[r:r + 8] for r in range(0, slabs, 8)))
                parts.append(_sublane_sums(prods, sub))
            else:
                u = rows[e0:e0 + 8, 0:slabs, :].astype(F32)
                parts.append(jnp.sum(u * ht[None], axis=1))
        part = jnp.concatenate(parts, axis=0)
        s = jnp.sum(part, axis=-1, keepdims=True)
        act = 0.5 * s * (1.0 + lax.erf(s * (2.0 ** -0.5)))
        gate_row = gate_ref[pl.ds(t, 1), :]
        gate_col = jnp.sum(jnp.where(eye, gate_row, 0.0), axis=-1, keepdims=True)
        wbuf[...] = jnp.broadcast_to(act * gate_col, (n_sel, LANES))
        sums = [xt] + [jnp.zeros_like(xt)] * 3
        for e in range(n_sel):
            sums[e % 4] = sums[e % 4] + wbuf[e:e + 1, :] * rows[e, slabs:2 * slabs, :].astype(F32)
        o_ref[t] = (sums[0] + sums[1]) + (sums[2] + sums[3])
        return carry

    lax.fori_loop(0, tokens, token, 0)

    @pl.when(i == n_steps - 1)
    def _():
        for t in range(tokens):
            token_rows_wait(nxt, t)


def _peer_experts(x, norm_g, ids, gates, uv):
    n, d = x.shape
    n_sel = ids.shape[1]
    slabs = d // LANES
    tt = min(GATHER_TOKENS, n)
    steps = n // tt
    kern = functools.partial(_peer_expert_kernel, tokens=tt, n_sel=n_sel, slabs=slabs, d_model=d)
    ids3 = ids.reshape(steps, tt, n_sel)
    out = pl.pallas_call(
        kern,
        grid=(steps,),
        in_specs=[
            pl.BlockSpec((1, tt, n_sel), lambda i: (i, 0, 0), memory_space=pltpu.SMEM),
            pl.BlockSpec((1, tt, n_sel), lambda i: (jnp.minimum(i + 1, steps - 1), 0, 0),
                         memory_space=pltpu.SMEM),
            pl.BlockSpec((tt, slabs, LANES), lambda i: (i, 0, 0)),
            pl.BlockSpec((slabs, LANES), lambda i: (0, 0)),
            pl.BlockSpec((tt, n_sel), lambda i: (i, 0)),
            pl.BlockSpec(memory_space=pl.ANY),
        ],
        out_specs=pl.BlockSpec((tt, slabs, LANES), lambda i: (i, 0, 0)),
        out_shape=jax.ShapeDtypeStruct((n, slabs, LANES), F32),
        scratch_shapes=[
            pltpu.VMEM((2, tt * n_sel, 2 * slabs, LANES), uv.dtype),
            pltpu.VMEM((n_sel, LANES), F32),
            pltpu.SemaphoreType.DMA((2, tt)),
        ],
        compiler_params=_cparams("arbitrary"),
        name="peer_experts",
    )(ids3, ids3, x.reshape(n, slabs, LANES), norm_g.reshape(slabs, LANES), gates, uv)
    return out.reshape(n, d)


def _peer_ffn(x, norm_g, w_q, keys_1, keys_2, table_u, table_v):
    n, d = x.shape
    n_experts = table_u.shape[0]
    q = _norm_matmul(x, norm_g, w_q, BF16)
    ids, gates = _peer_route(q, keys_1, keys_2, n_experts)
    ids = jnp.swapaxes(ids, 1, 2).reshape(n, -1)
    gates = jnp.swapaxes(gates, 1, 2).reshape(n, -1)
    slabs = d // LANES
    uv = jnp.concatenate([table_u.reshape(n_experts, slabs, LANES),
                          table_v.reshape(n_experts, slabs, LANES)], axis=1).astype(BF16)
    return _peer_experts(x, norm_g, ids, gates, uv)


def _tile_cols(w, reps):
    return jnp.concatenate([w] * reps, axis=1)


def kernel(x, positions, sb_norm_g, sb_w_qkv, sb_w_o, kv_norm_g, mla_w_dkv, mla_kv_latent_g,
           mla_w_ukv, mla_norm_g, mla_w_dq, mla_q_latent_g, mla_w_uq, mla_w_o, peer_norm_g,
           peer_w_q, peer_keys_1, peer_keys_2, peer_u, peer_v, final_norm_g):
    b, s, d = x.shape
    n = b * s
    depth = peer_norm_g.shape[0]
    n_a = sb_norm_g.shape[0]
    xf = x.reshape(n, d)
    pos = positions.reshape(n, 1).astype(F32)
    inv_freq = ROPE_BASE ** (-jnp.arange(0, ROPE_DIM, 2, dtype=F32) / ROPE_DIM)
    freq = jnp.tile(inv_freq, ROPE_GROUP).reshape(1, LANES)

    kv = kr1 = kr2 = None
    for layer in range(depth):
        if layer < n_a:
            heads = sb_w_qkv.shape[2] // (3 * HEAD_DIM)
            qkv = _norm_matmul(xf, sb_norm_g[layer], sb_w_qkv[layer], BF16)
            o = _sb_attention(qkv.reshape(b, s, -1), heads)
            xf = _matmul_residual(o.reshape(n, -1), sb_w_o[layer], xf)
        else:
            kv_lora = mla_kv_latent_g.shape[0]
            heads = mla_w_ukv.shape[1] // (2 * HEAD_DIM)
            if layer == n_a:
                w_dkv = jnp.concatenate([
                    mla_w_dkv[:, :kv_lora],
                    _tile_cols(mla_w_dkv[:, kv_lora:kv_lora + ROPE_HALF], ROPE_GROUP),
                    _tile_cols(mla_w_dkv[:, kv_lora + ROPE_HALF:], ROPE_GROUP)], axis=1)
                ckv = _norm_matmul(xf, kv_norm_g, w_dkv, F32)
                kv = _norm_matmul(ckv, mla_kv_latent_g, mla_w_ukv, BF16, k=kv_lora).reshape(b, s, -1)
                blk = kv_lora // LANES
                kr1, kr2 = _rope(ckv, blk, blk + 1, 1, pos, freq)
                kr1 = kr1.reshape(b, s, LANES)
                kr2 = kr2.reshape(b, s, LANES)
            j = layer - n_a
            q_lora = mla_q_latent_g.shape[1]
            cq = _norm_matmul(xf, mla_norm_g[j], mla_w_dq[j], F32)
            w_uq = mla_w_uq[j].reshape(q_lora, heads, HEAD_DIM + ROPE_DIM)
            w_all = jnp.concatenate([
                w_uq[:, :, :HEAD_DIM].reshape(q_lora, -1),
                w_uq[:, :, HEAD_DIM:HEAD_DIM + ROPE_HALF].reshape(q_lora, -1),
                w_uq[:, :, HEAD_DIM + ROPE_HALF:].reshape(q_lora, -1)], axis=1)
            qall = _norm_matmul(cq, mla_q_latent_g[j], w_all, BF16)
            nope_blk = heads * HEAD_DIM // LANES
            rope_blk = heads * ROPE_HALF // LANES
            qr1, qr2 = _rope(qall, nope_blk, nope_blk + rope_blk, rope_blk, pos, freq)
            o = _mla_attention(qall.reshape(b, s, -1), qr1.reshape(b, s, -1), qr2.reshape(b, s, -1),
                               kv, kr1, kr2, heads)
            xf = _matmul_residual(o.reshape(n, -1), mla_w_o[j], xf)
        xf = _peer_ffn(xf, peer_norm_g[layer], peer_w_q[layer], peer_keys_1[layer],
                       peer_keys_2[layer], peer_u[layer], peer_v[layer])
    return _rmsnorm(xf, final_norm_g).reshape(b, s, d)
```
